```python
import jax, jax.numpy as jnp
from jax import lax
import numpy as np

D_MODEL = 1024
BATCH = 32
SEQ = 2048
DEPTH = 2

EPS = 1e-6
N_BRANCH = 3
BRANCH_WIDTH = 512

ATTN_HEADS = 8
ATTN_KV_HEADS = 2
ATTN_HEAD_DIM = 64
ATTN_WIDTH = ATTN_HEADS * ATTN_HEAD_DIM
ATTN_KV_WIDTH = ATTN_KV_HEADS * ATTN_HEAD_DIM
WINDOW = 128
ATTN_BLOCK = 128
ROPE_THETA = 10000.0

GLA_HEADS = 4
GLA_KEY_WIDTH = 256
GLA_VALUE_WIDTH = 512
GLA_DK = GLA_KEY_WIDTH // GLA_HEADS
GLA_DV = GLA_VALUE_WIDTH // GLA_HEADS
GLA_GATE_RANK = 16
GLA_GATE_NORMALIZER = 16.0
GLA_CHUNK = 64

SSD_D_INNER = 512
SSD_HEAD_DIM = 64
SSD_HEADS = SSD_D_INNER // SSD_HEAD_DIM
SSD_GROUPS = 2
SSD_D_STATE = 128
SSD_CONV = 4
SSD_CONV_DIM = SSD_D_INNER + 2 * SSD_GROUPS * SSD_D_STATE
SSD_CHUNK = 64

IN_PROJ_SIZES = (
    ATTN_WIDTH, ATTN_KV_WIDTH, ATTN_KV_WIDTH, ATTN_WIDTH,
    GLA_KEY_WIDTH, GLA_KEY_WIDTH, GLA_VALUE_WIDTH, GLA_VALUE_WIDTH,
    GLA_GATE_RANK,
    SSD_CONV_DIM, SSD_HEADS, SSD_D_INNER,
    N_BRANCH * D_MODEL,
)
IN_PROJ_DIM = sum(IN_PROJ_SIZES)

kernel_name = "hybrid_swa_gla_ssd_gated_merge"


def rms_norm(x, g):
    xf = x.astype(jnp.float32)
    y = xf * lax.rsqrt(jnp.mean(xf * xf, axis=-1, keepdims=True) + EPS)
    return (y * g.astype(jnp.float32)).astype(x.dtype)


def rope_tables(positions):
    inv_freq = ROPE_THETA ** (-jnp.arange(0, ATTN_HEAD_DIM, 2, dtype=jnp.float32) / ATTN_HEAD_DIM)
    ang = positions.astype(jnp.float32)[..., None] * inv_freq
    return jnp.cos(ang)[:, :, None, :], jnp.sin(ang)[:, :, None, :]


def apply_rope(t, cos, sin):
    tf = t.astype(jnp.float32)
    t1, t2 = jnp.split(tf, 2, axis=-1)
    return jnp.concatenate([t1 * cos - t2 * sin, t2 * cos + t1 * sin], axis=-1).astype(t.dtype)


def sliding_window_attention(q, k, v, sinks):
    b, s = q.shape[0], q.shape[1]
    nb = s // ATTN_BLOCK
    grp = ATTN_HEADS // ATTN_KV_HEADS
    qb = q.reshape(b, nb, ATTN_BLOCK, ATTN_KV_HEADS, grp, ATTN_HEAD_DIM)

    def band(t):
        tp = jnp.pad(t, ((0, 0), (ATTN_BLOCK, 0), (0, 0), (0, 0)))
        prev = tp[:, :s].reshape(b, nb, ATTN_BLOCK, ATTN_KV_HEADS, ATTN_HEAD_DIM)
        cur = t.reshape(b, nb, ATTN_BLOCK, ATTN_KV_HEADS, ATTN_HEAD_DIM)
        return jnp.concatenate([prev, cur], axis=2)

    kb, vb = band(k), band(v)
    scale = ATTN_HEAD_DIM ** -0.5
    scores = jnp.einsum("bnqhgd,bnkhd->bnhgqk", qb, kb).astype(jnp.float32) * scale
    qi = jnp.arange(ATTN_BLOCK)[:, None]
    ki = jnp.arange(2 * ATTN_BLOCK)[None, :]
    rel = qi + ATTN_BLOCK - ki
    key_pos = jnp.arange(nb)[:, None, None] * ATTN_BLOCK - ATTN_BLOCK + ki
    mask = (rel >= 0) & (rel < WINDOW) & (key_pos >= 0)
    scores = jnp.where(mask[None, :, None, None], scores, jnp.finfo(jnp.float32).min)
    sink = jnp.broadcast_to(sinks.astype(jnp.float32).reshape(1, 1, ATTN_KV_HEADS, grp, 1, 1),
                            scores.shape[:-1] + (1,))
    probs = jax.nn.softmax(jnp.concatenate([scores, sink], axis=-1), axis=-1)[..., :-1]
    out = jnp.einsum("bnhgqk,bnkhd->bnqhgd", probs.astype(v.dtype), vb)
    return out.reshape(b, s, ATTN_WIDTH)


def gated_linear_attention(q, k, v, log_alpha):
    b, s, nh, dk = q.shape
    dv = v.shape[-1]
    c = GLA_CHUNK
    nc = s // c
    f32 = jnp.float32
    qc = (q.astype(f32) * dk ** -0.5).reshape(b, nc, c, nh, dk)
    kc = k.astype(f32).reshape(b, nc, c, nh, dk)
    vc = v.astype(f32).reshape(b, nc, c, nh, dv)
    bcum = jnp.cumsum(log_alpha.astype(f32).reshape(b, nc, c, nh, dk), axis=2)
    q_dec = qc * jnp.exp(bcum)
    k_inv = kc * jnp.exp(-bcum)
    causal = jnp.tril(jnp.ones((c, c), bool))
    attn = jnp.where(causal, jnp.einsum("bnihd,bnjhd->bnhij", q_dec, k_inv), 0.0)
    o_intra = jnp.einsum("bnhij,bnjhv->bnihv", attn, vc)
    b_last = bcum[:, :, -1]
    k_end = kc * jnp.exp(b_last[:, :, None] - bcum)
    chunk_states = jnp.einsum("bnjhd,bnjhv->bnhdv", k_end, vc)
    chunk_decay = jnp.exp(b_last)

    def step(state, inp):
        dec, upd = inp
        return dec[..., None] * state + upd, state

    init = jnp.zeros((b, nh, dk, dv), f32)
    _, prev = lax.scan(step, init, (jnp.moveaxis(chunk_decay, 1, 0), jnp.moveaxis(chunk_states, 1, 0)))
    prev = jnp.moveaxis(prev, 0, 1)
    o_inter = jnp.einsum("bnihd,bnhdv->bnihv", q_dec, prev)
    return (o_intra + o_inter).reshape(b, s, nh, dv)


def causal_depthwise_conv(u, w, bias):
    kw = w.shape[0]
    s = u.shape[1]
    up = jnp.pad(u, ((0, 0), (kw - 1, 0), (0, 0)))
    out = bias
    for i in range(kw):
        out = out + up[:, i:i + s] * w[i]
    return out


def ssd_scan(x, dt, a_head, bmat, cmat):
    b, s, nh, p = x.shape
    g, n = bmat.shape[2], bmat.shape[3]
    hpg = nh // g
    l = SSD_CHUNK
    nc = s // l
    f32 = jnp.float32
    dtf = dt.astype(f32)
    xc = (x.astype(f32) * dtf[..., None]).reshape(b, nc, l, g, hpg, p)
    a_cs = jnp.cumsum((dtf * a_head.astype(f32)).reshape(b, nc, l, g, hpg), axis=2)
    bc = bmat.astype(f32).reshape(b, nc, l, g, n)
    cc = cmat.astype(f32).reshape(b, nc, l, g, n)
    tril = jnp.tril(jnp.ones((l, l), bool))
    seg = a_cs[:, :, :, None] - a_cs[:, :, None, :]
    decay_ls = jnp.exp(jnp.where(tril[:, :, None, None], seg, -jnp.inf))
    cb = jnp.einsum("bclgn,bcsgn->bclsg", cc, bc)
    y_diag = jnp.einsum("bclsgh,bcsghp->bclghp", cb[..., None] * decay_ls, xc)
    decay_to_end = jnp.exp(a_cs[:, :, -1:] - a_cs)
    chunk_states = jnp.einsum("bclgn,bclghp->bcghpn", bc, xc * decay_to_end[..., None])
    chunk_decay = jnp.exp(a_cs[:, :, -1])

    def step(state, inp):
        dec, upd = inp
        return dec[..., None, None] * state + upd, state

    init = jnp.zeros((b, g, hpg, p, n), f32)
    _, prev = lax.scan(step, init, (jnp.moveaxis(chunk_decay, 1, 0), jnp.moveaxis(chunk_states, 1, 0)))
    prev = jnp.moveaxis(prev, 0, 1)
    y_off = jnp.einsum("bclgn,bcghpn->bclghp", cc, prev) * jnp.exp(a_cs)[..., None]
    return (y_diag + y_off).reshape(b, s, nh, p)


def gated_group_rms_norm(y, z, g, groups):
    shp = y.shape
    u = (y.astype(jnp.float32) * jax.nn.silu(z.astype(jnp.float32)))
    u = u.reshape(shp[:-1] + (groups, shp[-1] // groups))
    u = u * lax.rsqrt(jnp.mean(u * u, axis=-1, keepdims=True) + EPS)
    return (u.reshape(shp) * g.astype(jnp.float32)).astype(y.dtype)


def hybrid_layer(x, cos, sin, norm_g, w_in, attn_q_norm, attn_k_norm, attn_sinks,
                 gla_w_gate_up, gla_b_gate, gla_out_norm, ssd_conv_w, ssd_conv_b,
                 ssd_dt_bias, ssd_A_log, ssd_D, ssd_out_norm, w_branch, w_out):
    b, s, _ = x.shape
    h = rms_norm(x, norm_g)
    proj = h @ w_in
    split_points = np.cumsum(IN_PROJ_SIZES)[:-1].tolist()
    (aq, ak, av, ag, gq, gk, gv, gg, gdown, sxbc, sdt, sz, mg) = jnp.split(proj, split_points, axis=-1)

    q = apply_rope(rms_norm(aq.reshape(b, s, ATTN_HEADS, ATTN_HEAD_DIM), attn_q_norm), cos, sin)
    k = apply_rope(rms_norm(ak.reshape(b, s, ATTN_KV_HEADS, ATTN_HEAD_DIM), attn_k_norm), cos, sin)
    v = av.reshape(b, s, ATTN_KV_HEADS, ATTN_HEAD_DIM)
    y_a = sliding_window_attention(q, k, v, attn_sinks) * jax.nn.silu(ag)

    gate_logit = (gdown @ gla_w_gate_up + gla_b_gate).astype(jnp.float32)
    log_alpha = jax.nn.log_sigmoid(gate_logit) / GLA_GATE_NORMALIZER
    o_b = gated_linear_attention(gq.reshape(b, s, GLA_HEADS, GLA_DK),
                                 gk.reshape(b, s, GLA_HEADS, GLA_DK),
                                 gv.reshape(b, s, GLA_HEADS, GLA_DV),
                                 log_alpha.reshape(b, s, GLA_HEADS, GLA_DK)).astype(x.dtype)
    y_b = rms_norm(o_b, gla_out_norm).reshape(b, s, GLA_VALUE_WIDTH) * jax.nn.silu(gg)

    xbc = jax.nn.silu(causal_depthwise_conv(sxbc, ssd_conv_w, ssd_conv_b))
    sx, sb, sc = jnp.split(xbc, [SSD_D_INNER, SSD_D_INNER + SSD_GROUPS * SSD_D_STATE], axis=-1)
    dt = jax.nn.softplus((sdt + ssd_dt_bias).astype(jnp.float32))
    a_head = -jnp.exp(ssd_A_log.astype(jnp.float32))
    xh = sx.reshape(b, s, SSD_HEADS, SSD_HEAD_DIM)
    y_c = ssd_scan(xh, dt, a_head,
                   sb.reshape(b, s, SSD_GROUPS, SSD_D_STATE),
                   sc.reshape(b, s, SSD_GROUPS, SSD_D_STATE))
    y_c = (y_c + xh.astype(jnp.float32) * ssd_D.astype(jnp.float32)[:, None]).astype(x.dtype)
    y_c = gated_group_rms_norm(y_c.reshape(b, s, SSD_D_INNER), sz, ssd_out_norm, SSD_GROUPS)

    branches = jnp.stack([y_a, y_b, y_c], axis=2)
    u = jnp.einsum("bsnw,nwd->bsnd", branches, w_branch)
    gates = jax.nn.sigmoid(mg.reshape(b, s, N_BRANCH, D_MODEL))
    merged = jnp.sum(gates * u, axis=2)
    return x + merged @ w_out


def setup_inputs(seed: int = 0) -> dict:
    key = jax.random.key(seed)
    ks = jax.random.split(key, 20)
    f32 = jnp.float32
    nrm = lambda k, shp: jax.random.normal(k, shp, f32)
    x = nrm(ks[0], (BATCH, SEQ, D_MODEL))
    offset = jax.random.randint(ks[1], (BATCH, 1), 0, 4096, dtype=jnp.int32)
    positions = (jnp.arange(SEQ, dtype=jnp.int32)[None, :] + offset).astype(jnp.int32)
    norm_g = 1.0 + 0.02 * nrm(ks[2], (DEPTH, D_MODEL))
    w_in = nrm(ks[3], (DEPTH, D_MODEL, IN_PROJ_DIM)) * D_MODEL ** -0.5
    attn_q_norm = 1.0 + 0.02 * nrm(ks[4], (DEPTH, ATTN_HEAD_DIM))
    attn_k_norm = 1.0 + 0.02 * nrm(ks[5], (DEPTH, ATTN_HEAD_DIM))
    attn_sinks = 0.5 * nrm(ks[6], (DEPTH, ATTN_HEADS))
    gla_w_gate_up = nrm(ks[7], (DEPTH, GLA_GATE_RANK, GLA_KEY_WIDTH)) * GLA_GATE_RANK ** -0.5
    gla_b_gate = 0.02 * nrm(ks[8], (DEPTH, GLA_KEY_WIDTH))
    gla_out_norm = 1.0 + 0.02 * nrm(ks[9], (DEPTH, GLA_DV))
    ssd_conv_w = nrm(ks[10], (DEPTH, SSD_CONV, SSD_CONV_DIM)) * SSD_CONV ** -0.5
    ssd_conv_b = 0.02 * nrm(ks[11], (DEPTH, SSD_CONV_DIM))
    dt0 = jnp.exp(jax.random.uniform(ks[12], (DEPTH, SSD_HEADS), f32, np.log(1e-3), np.log(1e-1)))
    ssd_dt_bias = dt0 + jnp.log(-jnp.expm1(-dt0))
    ssd_A_log = jnp.log(jax.random.uniform(ks[13], (DEPTH, SSD_HEADS), f32, 1.0, 16.0))
    ssd_D = 1.0 + 0.02 * nrm(ks[14], (DEPTH, SSD_HEADS))
    ssd_out_norm = 1.0 + 0.02 * nrm(ks[15], (DEPTH, SSD_D_INNER))
    w_branch = nrm(ks[16], (DEPTH, N_BRANCH, BRANCH_WIDTH, D_MODEL)) * BRANCH_WIDTH ** -0.5
    w_out = nrm(ks[17], (DEPTH, D_MODEL, D_MODEL)) * D_MODEL ** -0.5
    return {"x": x, "positions": positions, "norm_g": norm_g, "w_in": w_in,
            "attn_q_norm": attn_q_norm, "attn_k_norm": attn_k_norm, "attn_sinks": attn_sinks,
            "gla_w_gate_up": gla_w_gate_up, "gla_b_gate": gla_b_gate, "gla_out_norm": gla_out_norm,
            "ssd_conv_w": ssd_conv_w, "ssd_conv_b": ssd_conv_b, "ssd_dt_bias": ssd_dt_bias,
            "ssd_A_log": ssd_A_log, "ssd_D": ssd_D, "ssd_out_norm": ssd_out_norm,
            "w_branch": w_branch, "w_out": w_out}


def reference(x, positions, norm_g, w_in, attn_q_norm, attn_k_norm, attn_sinks,
              gla_w_gate_up, gla_b_gate, gla_out_norm, ssd_conv_w, ssd_conv_b,
              ssd_dt_bias, ssd_A_log, ssd_D, ssd_out_norm, w_branch, w_out):
    cos, sin = rope_tables(positions)
    for i in range(DEPTH):
        x = hybrid_layer(x, cos, sin, norm_g[i], w_in[i], attn_q_norm[i], attn_k_norm[i],
                         attn_sinks[i], gla_w_gate_up[i], gla_b_gate[i], gla_out_norm[i],
                         ssd_conv_w[i], ssd_conv_b[i], ssd_dt_bias[i], ssd_A_log[i], ssd_D[i],
                         ssd_out_norm[i], w_branch[i], w_out[i])
    return x
```

```python
import functools

import numpy as np
import jax
import jax.numpy as jnp
from jax import lax
from jax.experimental import pallas as pl
from jax.experimental.pallas import tpu as pltpu

F32 = jnp.float32
BF16 = jnp.bfloat16

D_MODEL = 1024
EPS = 1e-6
ROPE_THETA = 10000.0

ATTN_HEADS = 8
ATTN_KV_HEADS = 2
ATTN_HEAD_DIM = 64
ATTN_BLOCK = 128
GLA_HEADS = 4
GLA_DK = 64
GLA_DV = 128
GLA_GATE_RANK = 16
GLA_GATE_NORMALIZER = 16.0
SSD_HEADS = 8
SSD_HEAD_DIM = 64
SSD_GROUPS = 2
SSD_D_STATE = 128
SSD_D_INNER = 512
SSD_CONV = 4

IN_PROJ_SIZES = (512, 128, 128, 512, 256, 256, 512, 512, 16, 1024, 8, 512, 3072)
IN_PROJ_CUTS = tuple(int(v) for v in np.cumsum((0,) + IN_PROJ_SIZES))

LANES = 128
CHUNK = 128
CONV_PAD = 8
SEQ_BLOCK = 512
VMEM_LIMIT_BYTES = 56 * 1024 * 1024
MASK_VALUE = -1e30


def _mm(a, b):
    return jnp.dot(a, b, preferred_element_type=F32)


def _mm_nt(a, b):
    return lax.dot_general(a, b, (((1,), (1,)), ((), ())), preferred_element_type=F32)


def _mm_tn(a, b):
    return lax.dot_general(a, b, (((0,), (0,)), ((), ())), preferred_element_type=F32)


def _sigmoid(v):
    return 0.5 * jnp.tanh(0.5 * v) + 0.5


def _silu(v):
    return v * _sigmoid(v)


def _softplus(v):
    return jnp.maximum(v, 0.0) + jnp.log(1.0 + jnp.exp(-jnp.abs(v)))


def _split3(v):
    hi = v.astype(BF16)
    r1 = v - hi.astype(F32)
    mid = r1.astype(BF16)
    lo = (r1 - mid.astype(F32)).astype(BF16)
    return hi, mid, lo


def _rope_table_kernel(pos_ref, freq_ref, cos_ref, sin_ref):
    ang = pos_ref[0].astype(F32) * freq_ref[...]
    lane = lax.broadcasted_iota(jnp.int32, ang.shape, 1)
    cos_ref[0] = jnp.cos(ang)
    sin_ref[0] = jnp.where(lane < LANES // 2, -jnp.sin(ang), jnp.sin(ang))


def _layer_kernel(
        sinks_ref, x_ref, cos_ref, sin_ref, ng_ref,
        wq_ref, wkv_ref, wga_ref, wgqk_ref, wgv_ref, wgg_ref, wsm_ref, wxbc_ref, wz_ref,
        wm_ref, wbr_ref, wo_ref,
        qgain_ref, kgain_ref, gmat_ref, wup_ref, bgate_ref, gnorm_ref, ltri_ref, emat_ref,
        convw_ref, convb_ref, dtb_ref, alog_ref, dexp_ref, snorm_ref,
        out_ref,
        hb_scr, q_scr, k_scr, v_scr, br_scr, merged_scr, conv_scr, gla_scr, ssd_scr,
        *, ts):
    nblk = ts // ATTN_BLOCK
    nchunk = ts // CHUNK
    j = pl.program_id(1)

    @pl.when(j == 0)
    def _reset_carries():
        k_scr[0:ATTN_BLOCK, :] = jnp.zeros((ATTN_BLOCK, LANES), BF16)
        v_scr[0:ATTN_BLOCK, :] = jnp.zeros((ATTN_BLOCK, LANES), BF16)
        conv_scr[0:CONV_PAD, :] = jnp.zeros((CONV_PAD, conv_scr.shape[1]), F32)
        gla_scr[...] = jnp.zeros(gla_scr.shape, F32)
        ssd_scr[...] = jnp.zeros(ssd_scr.shape, F32)

    x = x_ref[0]
    ms = jnp.mean(x * x, axis=-1, keepdims=True)
    hb_scr[...] = (x * lax.rsqrt(ms + EPS) * ng_ref[...]).astype(BF16)
    hb = hb_scr[...]

    lane = lax.broadcasted_iota(jnp.int32, (ATTN_BLOCK, LANES), 1)
    low_half = lane < LANES // 2

    cos_t = cos_ref[0]
    sin_t = sin_ref[0]
    gmat = gmat_ref[...]
    akv = _mm(hb, wkv_ref[...])
    ak = akv[:, :LANES]
    kn = ak * lax.rsqrt(_mm((ak * ak).astype(BF16), gmat) + EPS) * kgain_ref[...]
    kr = kn * cos_t + pltpu.roll(kn, LANES // 2, 1) * sin_t
    k_scr[ATTN_BLOCK:ATTN_BLOCK + ts, :] = kr.astype(BF16)
    v_scr[ATTN_BLOCK:ATTN_BLOCK + ts, :] = akv[:, LANES:].astype(BF16)

    aq = _mm(hb, wq_ref[...])
    lane_ts = lax.broadcasted_iota(jnp.int32, (ts, LANES), 1)
    group0 = (lane_ts // 32) % 2 == 0
    qgain = qgain_ref[...] * (ATTN_HEAD_DIM ** -0.5)
    for c in range(4):
        qc = aq[:, c * LANES:(c + 1) * LANES]
        qn = qc * lax.rsqrt(_mm((qc * qc).astype(BF16), gmat) + EPS) * qgain
        qr = qn * cos_t + pltpu.roll(qn, LANES // 2, 1) * sin_t
        q_scr[0, c * ts:(c + 1) * ts, :] = jnp.where(group0, qr, 0.0).astype(BF16)
        q_scr[1, c * ts:(c + 1) * ts, :] = jnp.where(group0, 0.0, qr).astype(BF16)

    rows4 = 4 * ATTN_BLOCK
    qi = lax.broadcasted_iota(jnp.int32, (rows4, 2 * ATTN_BLOCK), 0) % ATTN_BLOCK
    ki = lax.broadcasted_iota(jnp.int32, (rows4, 2 * ATTN_BLOCK), 1)
    band = (ki > qi) & (ki <= qi + ATTN_BLOCK)
    band_first = band & ((ki >= ATTN_BLOCK) | (j > 0))
    row_head = lax.broadcasted_iota(jnp.int32, (rows4, 1), 0) // ATTN_BLOCK
    sink_cols = []
    for g in range(ATTN_KV_HEADS):
        col = jnp.full((rows4, 1), sinks_ref[4 * g + 3], F32)
        for c in range(2, -1, -1):
            col = jnp.where(row_head == c, sinks_ref[4 * g + c], col)
        sink_cols.append(col)

    for n in range(nblk):
        kb = k_scr[n * ATTN_BLOCK:(n + 2) * ATTN_BLOCK, :]
        vb = v_scr[n * ATTN_BLOCK:(n + 2) * ATTN_BLOCK, :]
        outs = []
        for g in range(ATTN_KV_HEADS):
            lhs = jnp.concatenate(
                [q_scr[g, c * ts + n * ATTN_BLOCK:c * ts + (n + 1) * ATTN_BLOCK, :] for c in range(4)],
                axis=0)
            s = jnp.where(band_first if n == 0 else band, _mm_nt(lhs, kb), MASK_VALUE)
            m = jnp.maximum(jnp.max(s, axis=1, keepdims=True), sink_cols[g])
            p = jnp.exp(s - m)
            denom = jnp.sum(p, axis=1, keepdims=True) + jnp.exp(sink_cols[g] - m)
            outs.append(_mm(p.astype(BF16), vb) / denom)
        for c in range(4):
            br_scr[n * ATTN_BLOCK:(n + 1) * ATTN_BLOCK, c * LANES:(c + 1) * LANES] = jnp.where(
                low_half,
                outs[0][c * ATTN_BLOCK:(c + 1) * ATTN_BLOCK],
                outs[1][c * ATTN_BLOCK:(c + 1) * ATTN_BLOCK])
    k_scr[0:ATTN_BLOCK, :] = k_scr[ts:ts + ATTN_BLOCK, :]
    v_scr[0:ATTN_BLOCK, :] = v_scr[ts:ts + ATTN_BLOCK, :]

    y_a = br_scr[...] * _silu(_mm(hb, wga_ref[...]))
    merged_scr[...] = _sigmoid(_mm(hb, wm_ref[:, 0:D_MODEL])) * _mm(y_a.astype(BF16), wbr_ref[0])

    psm = _mm(hb, wsm_ref[...])
    logit = _mm(psm.astype(BF16), wup_ref[...]) + bgate_ref[...]
    log_alpha = (jnp.minimum(logit, 0.0) - jnp.log(1.0 + jnp.exp(-jnp.abs(logit)))) * (
        1.0 / GLA_GATE_NORMALIZER)
    dt = _softplus(psm + dtb_ref[...])
    adt = dt * (-jnp.exp(alog_ref[...]))
    cs_in = jnp.concatenate([log_alpha, adt], axis=1)
    ltri = ltri_ref[...]
    cs_rows = ltri.shape[0]
    cs_parts = []
    for r in range(ts // cs_rows):
        pieces = _split3(cs_in[r * cs_rows:(r + 1) * cs_rows])
        cs_parts.append(_mm(ltri, pieces[0]) + _mm(ltri, pieces[1]) + _mm(ltri, pieces[2]))
    cs = cs_parts[0] if len(cs_parts) == 1 else jnp.concatenate(cs_parts, axis=0)
    bcum = cs[:, :GLA_HEADS * GLA_DK]
    acs = cs[:, GLA_HEADS * GLA_DK:]

    gqk = _mm(hb, wgqk_ref[...])
    gv = _mm(hb, wgv_ref[...])
    dkw = GLA_HEADS * GLA_DK
    lane_k = lax.broadcasted_iota(jnp.int32, (CHUNK, dkw), 1) // GLA_DK
    ci = lax.broadcasted_iota(jnp.int32, (GLA_HEADS * CHUNK, CHUNK), 0) % CHUNK
    cj = lax.broadcasted_iota(jnp.int32, (GLA_HEADS * CHUNK, CHUNK), 1)
    causal4 = ci >= cj
    st_blockdiag = (lax.broadcasted_iota(jnp.int32, gla_scr.shape, 0) // GLA_DV
                    == lax.broadcasted_iota(jnp.int32, gla_scr.shape, 1) // GLA_DK)
    gnorm = gnorm_ref[...]
    for c in range(nchunk):
        rs = slice(c * CHUNK, (c + 1) * CHUNK)
        bc = bcum[rs]
        bmid = bc[CHUNK // 2 - 1:CHUNK // 2]
        blast = bc[CHUNK - 1:CHUNK]
        q = gqk[rs, :dkw] * (GLA_DK ** -0.5)
        k = gqk[rs, dkw:]
        v = gv[rs].astype(BF16)
        q_dec = (q * jnp.exp(bc)).astype(BF16)
        q_mid = q * jnp.exp(bc - bmid)
        k_mid = (k * jnp.exp(bmid - bc)).astype(BF16)
        k_end = (k * jnp.exp(blast - bc)).astype(BF16)
        lhs = jnp.concatenate(
            [jnp.where(lane_k == h, q_mid, 0.0).astype(BF16) for h in range(GLA_HEADS)], axis=0)
        attn = jnp.where(causal4, _mm_nt(lhs, k_mid), 0.0).astype(BF16)
        state_t = gla_scr[...]
        o_inter = _mm_nt(q_dec, state_t.astype(BF16))
        for h in range(GLA_HEADS):
            hs = slice(h * GLA_DV, (h + 1) * GLA_DV)
            o = _mm(attn[h * CHUNK:(h + 1) * CHUNK], v[:, hs]) + o_inter[:, hs]
            o = o * lax.rsqrt(jnp.mean(o * o, axis=-1, keepdims=True) + EPS)
            br_scr[rs, hs] = o
        gla_scr[...] = jnp.where(st_blockdiag, state_t * jnp.exp(blast) + _mm_tn(v, k_end), 0.0)

    y_b = br_scr[...] * gnorm * _silu(_mm(hb, wgg_ref[...]))
    merged_scr[...] += (_sigmoid(_mm(hb, wm_ref[:, D_MODEL:2 * D_MODEL]))
                        * _mm(y_b.astype(BF16), wbr_ref[1]))

    conv_scr[CONV_PAD:CONV_PAD + ts, :] = _mm(hb, wxbc_ref[...])
    acc = convb_ref[...] + convw_ref[SSD_CONV - 1:SSD_CONV, :] * conv_scr[CONV_PAD:CONV_PAD + ts, :]
    for i in range(SSD_CONV - 1):
        off = CONV_PAD - (SSD_CONV - 1) + i
        acc = acc + convw_ref[i:i + 1, :] * conv_scr[off:off + ts, :]
    conv_scr[0:CONV_PAD, :] = conv_scr[ts:ts + CONV_PAD, :]
    xbc = _silu(acc)
    nstate = SSD_GROUPS * SSD_D_STATE
    xs = xbc[:, :SSD_D_INNER]
    bmat = xbc[:, SSD_D_INNER:SSD_D_INNER + nstate].astype(BF16)
    cmat = xbc[:, SSD_D_INNER + nstate:]
    emat = emat_ref[...]
    tril = (lax.broadcasted_iota(jnp.int32, (CHUNK, CHUNK), 0)
            >= lax.broadcasted_iota(jnp.int32, (CHUNK, CHUNK), 1))
    gw = SSD_D_INNER // SSD_GROUPS
    for c in range(nchunk):
        rs = slice(c * CHUNK, (c + 1) * CHUNK)
        a = acs[rs]
        a_last = a[CHUNK - 1:CHUNK]
        w_hi = (dt[rs] * jnp.exp(a_last - a)).astype(BF16)
        w_lo = (dt[rs] * jnp.exp(a_last - a) - w_hi.astype(F32)).astype(BF16)
        xw = (xs[rs] * (_mm(w_hi, emat) + _mm(w_lo, emat))).astype(BF16)
        al3 = _split3(jnp.broadcast_to(a_last, (8, LANES)))
        chunk_decay = jnp.exp(_mm(al3[0], emat) + _mm(al3[1], emat) + _mm(al3[2], emat))[0:1]
        a_t = a.T
        dt_t = dt[rs].T
        x_c = xs[rs].astype(BF16)
        state_t = ssd_scr[...]
        state_b = state_t.astype(BF16)
        ys = []
        for g in range(SSD_GROUPS):
            gs = slice(g * SSD_D_STATE, (g + 1) * SSD_D_STATE)
            b_g = bmat[rs, gs]
            c_g = cmat[rs, gs]
            cb = _mm_nt(c_g.astype(BF16), b_g)
            for hh in range(SSD_HEADS // SSD_GROUPS):
                h = g * (SSD_HEADS // SSD_GROUPS) + hh
                a_col = a[:, h:h + 1]
                decay = jnp.where(tril, jnp.exp(a_col - a_t[h:h + 1, :]), 0.0)
                m_h = cb * decay * dt_t[h:h + 1, :]
                lhs = jnp.concatenate([m_h, c_g * jnp.exp(a_col)], axis=1).astype(BF16)
                ps = slice((h // 2) * LANES, (h // 2 + 1) * LANES)
                rhs = jnp.concatenate([x_c[:, ps], state_b[:, ps]], axis=0)
                ys.append(_mm(lhs, rhs))
            ws = slice(g * gw, (g + 1) * gw)
            ssd_scr[:, ws] = state_t[:, ws] * chunk_decay[:, ws] + _mm_tn(b_g, xw[:, ws])
        for p in range(SSD_HEADS // 2):
            br_scr[rs, p * LANES:(p + 1) * LANES] = jnp.where(low_half, ys[2 * p], ys[2 * p + 1])

    u = (br_scr[...] + xs * dexp_ref[...]) * _silu(_mm(hb, wz_ref[...]))
    normed = []
    for g in range(SSD_GROUPS):
        ug = u[:, g * gw:(g + 1) * gw]
        normed.append(ug * lax.rsqrt(jnp.mean(ug * ug, axis=-1, keepdims=True) + EPS))
    y_c = jnp.concatenate(normed, axis=1) * snorm_ref[...]
    merged = merged_scr[...] + (_sigmoid(_mm(hb, wm_ref[:, 2 * D_MODEL:3 * D_MODEL]))
                                * _mm(y_c.astype(BF16), wbr_ref[2]))

    out_ref[0] = x_ref[0] + _mm(merged.astype(BF16), wo_ref[...])


def _q_lane_perm():
    idx = np.empty((4, 4, 32), np.int64)
    for c in range(4):
        for r in range(4):
            idx[c, r] = (c + 4 * (r % 2)) * ATTN_HEAD_DIM + 32 * (r // 2) + np.arange(32)
    return idx.reshape(-1)


def _k_lane_perm():
    idx = np.empty((4, 32), np.int64)
    for r in range(4):
        idx[r] = (r % 2) * ATTN_HEAD_DIM + 32 * (r // 2) + np.arange(32)
    return idx.reshape(-1)


def _ya_lane_perm():
    idx = np.empty((4, 2, ATTN_HEAD_DIM), np.int64)
    for c in range(4):
        for half in range(2):
            idx[c, half] = (c + 4 * half) * ATTN_HEAD_DIM + np.arange(ATTN_HEAD_DIM)
    return idx.reshape(-1)


def _const_tables():
    lane = np.arange(LANES)
    gmat = ((lane[:, None] // 32) % 2 == (lane[None, :] // 32) % 2).astype(np.float32) / ATTN_HEAD_DIM
    r = np.arange(2 * CHUNK)
    ltri = ((r[:, None] // CHUNK == r[None, :] // CHUNK) & (r[:, None] >= r[None, :])).astype(np.float32)
    emat = np.zeros((LANES, SSD_D_INNER), np.float32)
    for h in range(SSD_HEADS):
        emat[h, h * SSD_HEAD_DIM:(h + 1) * SSD_HEAD_DIM] = 1.0
    return (jnp.asarray(gmat, BF16), jnp.asarray(ltri, BF16), jnp.asarray(emat, BF16))


def _row(v):
    return v.reshape(1, -1).astype(F32)


def _prep_layer(w_in, attn_q_norm, attn_k_norm, gla_w_gate_up, gla_b_gate, gla_out_norm,
                ssd_conv_w, ssd_conv_b, ssd_dt_bias, ssd_A_log, ssd_D, ssd_out_norm, w_branch, w_out):
    c = IN_PROJ_CUTS
    seg = lambda i: w_in[:, c[i]:c[i + 1]]
    qp, kp, yp = _q_lane_perm(), _k_lane_perm(), _ya_lane_perm()
    d_in = w_in.shape[0]
    wq = seg(0)[:, qp]
    wkv = jnp.concatenate([seg(1)[:, kp], seg(2)], axis=1)
    wga = seg(3)[:, yp]
    wgqk = jnp.concatenate([seg(4), seg(5)], axis=1)
    wgv, wgg = seg(6), seg(7)
    wsm = jnp.concatenate([seg(10), seg(8), jnp.zeros((d_in, LANES - 24), w_in.dtype)], axis=1)
    wxbc, wz, wm = seg(9), seg(11), seg(12)
    wbr = jnp.concatenate([w_branch[0:1, yp, :], w_branch[1:]], axis=0)
    pat = 32 * (np.arange(LANES) // 64) + np.arange(LANES) % 32
    wup = jnp.zeros((LANES, gla_w_gate_up.shape[1]), F32).at[8:8 + GLA_GATE_RANK].set(gla_w_gate_up)
    pad8 = lambda v: jnp.zeros((1, LANES), F32).at[0, :SSD_HEADS].set(v)
    big = [w.astype(BF16) for w in (wq, wkv, wga, wgqk, wgv, wgg, wsm, wxbc, wz, wm, wbr, w_out)]
    small = dict(
        qgain=_row(attn_q_norm[pat]), kgain=_row(attn_k_norm[pat]),
        wup=wup.astype(BF16), bgate=_row(gla_b_gate), gnorm=_row(jnp.tile(gla_out_norm, GLA_HEADS)),
        convw=ssd_conv_w.astype(F32), convb=_row(ssd_conv_b), dtb=pad8(ssd_dt_bias),
        alog=pad8(ssd_A_log), dexp=_row(jnp.repeat(ssd_D, SSD_HEAD_DIM)), snorm=_row(ssd_out_norm))
    return big, small


def _const_spec(shape):
    nd = len(shape)
    return pl.BlockSpec(shape, lambda b, j, _nd=nd: (0,) * _nd, pipeline_mode=pl.Buffered(1))


def _layer_call(x, cos_t, sin_t, sinks, norm_g, big, small, consts, ts):
    bsz, seq, d = x.shape
    gmat, ltri, emat = consts
    operands = [sinks.astype(F32), x, cos_t, sin_t, _row(norm_g), *big,
                small["qgain"], small["kgain"], gmat, small["wup"], small["bgate"], small["gnorm"],
                ltri, emat, small["convw"], small["convb"], small["dtb"], small["alog"],
                small["dexp"], small["snorm"]]
    tok = lambda w: pl.BlockSpec((1, ts, w), lambda b, j: (b, j, 0))
    in_specs = [pl.BlockSpec(memory_space=pltpu.SMEM), tok(d), tok(LANES), tok(LANES)]
    in_specs += [_const_spec(op.shape) for op in operands[4:]]
    conv_dim = small["convw"].shape[1]
    scratch = [
        pltpu.VMEM((ts, d), BF16),
        pltpu.VMEM((ATTN_KV_HEADS, 4 * ts, LANES), BF16),
        pltpu.VMEM((ts + ATTN_BLOCK, LANES), BF16),
        pltpu.VMEM((ts + ATTN_BLOCK, LANES), BF16),
        pltpu.VMEM((ts, 512), F32),
        pltpu.VMEM((ts, d), F32),
        pltpu.VMEM((ts + CONV_PAD, conv_dim), F32),
        pltpu.VMEM((GLA_HEADS * GLA_DV, GLA_HEADS * GLA_DK), F32),
        pltpu.VMEM((SSD_D_STATE, SSD_D_INNER), F32),
    ]
    return pl.pallas_call(
        functools.partial(_layer_kernel, ts=ts),
        grid=(bsz, seq // ts),
        in_specs=in_specs,
        out_specs=pl.BlockSpec((1, ts, d), lambda b, j: (b, j, 0)),
        out_shape=jax.ShapeDtypeStruct(x.shape, x.dtype),
        scratch_shapes=scratch,
        compiler_params=pltpu.CompilerParams(
            dimension_semantics=("arbitrary", "arbitrary"),
            vmem_limit_bytes=VMEM_LIMIT_BYTES),
        name="hybrid_layer",
    )(*operands)


def _rope_tables(positions):
    bsz, seq = positions.shape
    inv_freq = ROPE_THETA ** (-jnp.arange(0, ATTN_HEAD_DIM, 2, dtype=F32) / ATTN_HEAD_DIM)
    freq = jnp.tile(inv_freq, LANES // inv_freq.shape[0]).reshape(1, LANES)
    spec = pl.BlockSpec((1, seq, LANES), lambda b: (b, 0, 0))
    return pl.pallas_call(
        _rope_table_kernel,
        grid=(bsz,),
        in_specs=[pl.BlockSpec((1, seq, 1), lambda b: (b, 0, 0)),
                  pl.BlockSpec((1, LANES), lambda b: (0, 0))],
        out_specs=[spec, spec],
        out_shape=[jax.ShapeDtypeStruct((bsz, seq, LANES), F32)] * 2,
        name="rope_tables",
    )(positions.reshape(bsz, seq, 1), freq)


def kernel(x, positions, norm_g, w_in, attn_q_norm, attn_k_norm, attn_sinks, gla_w_gate_up, gla_b_gate,
           gla_out_norm, ssd_conv_w, ssd_conv_b, ssd_dt_bias, ssd_A_log, ssd_D, ssd_out_norm,
           w_branch, w_out):
    seq = x.shape[1]
    ts = min(SEQ_BLOCK, seq)
    assert seq % ts == 0 and ts % (2 * CHUNK) == 0 and x.shape[2] == D_MODEL
    cos_t, sin_t = _rope_tables(positions)
    consts = _const_tables()
    for i in range(w_in.shape[0]):
        big, small = _prep_layer(
            w_in[i], attn_q_norm[i], attn_k_norm[i], gla_w_gate_up[i], gla_b_gate[i], gla_out_norm[i],
            ssd_conv_w[i], ssd_conv_b[i], ssd_dt_bias[i], ssd_A_log[i], ssd_D[i], ssd_out_norm[i],
            w_branch[i], w_out[i])
        x = _layer_call(x, cos_t, sin_t, attn_sinks[i], norm_g[i], big, small, consts, ts)
    return x
```

```python
import functools

import numpy as np
import jax
import jax.numpy as jnp
from jax import lax
from jax.experimental import pallas as pl
from jax.experimental.pallas import tpu as pltpu

F32 = jnp.float32
BF16 = jnp.bfloat16

D_MODEL = 1024
EPS = 1e-6
ROPE_THETA = 10000.0

ATTN_HEADS = 8
ATTN_KV_HEADS = 2
ATTN_HEAD_DIM = 64
ATTN_BLOCK = 128
GLA_HEADS = 4
GLA_DK = 64
GLA_DV = 128
GLA_GATE_RANK = 16
GLA_GATE_NORMALIZER = 16.0
SSD_HEADS = 8
SSD_HEAD_DIM = 64
SSD_GROUPS = 2
SSD_D_STATE = 128
SSD_D_INNER = 512
SSD_CONV = 4

IN_PROJ_SIZES = (512, 128, 128, 512, 256, 256, 512, 512, 16, 1024, 8, 512, 3072)
IN_PROJ_CUTS = tuple(int(v) for v in np.cumsum((0,) + IN_PROJ_SIZES))

LANES = 128
CHUNK = 128
CONV_PAD = 8
SEQ_BLOCK = 512
VMEM_LIMIT_BYTES = 56 * 1024 * 1024
MASK_VALUE = -1e30
MXU_TILE = 256
SOFTMAX_ROWS = 64


def _mm(a, b):
    return jnp.dot(a, b, preferred_element_type=F32)


def _mm_nt(a, b):
    return lax.dot_general(a, b, (((1,), (1,)), ((), ())), preferred_element_type=F32)


def _mm_tn(a, b):
    return lax.dot_general(a, b, (((0,), (0,)), ((), ())), preferred_element_type=F32)


def _silu_of_double(vh):
    return vh * jnp.tanh(vh) + vh


def _softplus(v):
    return jnp.maximum(v, 0.0) + jnp.log(1.0 + jnp.exp(-jnp.abs(v)))


def _split3(v):
    hi = v.astype(BF16)
    r1 = v - hi.astype(F32)
    mid = r1.astype(BF16)
    lo = (r1 - mid.astype(F32)).astype(BF16)
    return hi, mid, lo


def _rope_table_kernel(pos_ref, freq_ref, cos_ref, sin_ref):
    ang = pos_ref[0].astype(F32) * freq_ref[...]
    lane = lax.broadcasted_iota(jnp.int32, ang.shape, 1)
    cos_ref[0] = jnp.cos(ang)
    sin_ref[0] = jnp.where(lane < LANES // 2, -jnp.sin(ang), jnp.sin(ang))


def _layer_kernel(
        sinks_ref, x_ref, cos_ref, sin_ref, ng_ref,
        wq_ref, wkv_ref, wga_ref, wgqk_ref, wgv_ref, wgg_ref, wsm_ref, wxbc_ref, wz_ref,
        wm_ref, wbr_ref, wo_ref,
        qgain_ref, kgain_ref, gmat_ref, wup_ref, bgate_ref, gnorm_ref, ltri_ref, emat_ref,
        convw_ref, convb_ref, dtb_ref, alog_ref, dexp_ref, snorm_ref,
        out_ref,
        hb_scr, q_scr, k_scr, v_scr, bra_scr, brb_scr, brc_scr, merged_scr, conv_scr, gla_scr, ssd_scr,
        gqk_scr, gv_scr, psm_scr, sga_scr, sgg_scr, sz_scr, gate_ac_scr, gate_b_scr,
        ya_scr, yb_scr,
        *, ts):
    nblk = ts // ATTN_BLOCK
    nchunk = ts // CHUNK
    j = pl.program_id(1)

    @pl.when(j == 0)
    def _reset_carries():
        k_scr[0:ATTN_BLOCK, :] = jnp.zeros((ATTN_BLOCK, LANES), BF16)
        v_scr[0:ATTN_BLOCK, :] = jnp.zeros((ATTN_BLOCK, LANES), BF16)
        conv_scr[0:CONV_PAD, :] = jnp.zeros((CONV_PAD, conv_scr.shape[1]), F32)
        gla_scr[...] = jnp.zeros(gla_scr.shape, F32)
        ssd_scr[...] = jnp.zeros(ssd_scr.shape, F32)

    def proj_tile(w_ref, c0, dst_ref, d0, fn=None, width=MXU_TILE):
        def run():
            t = _mm(hb_scr[...], w_ref[:, c0:c0 + width])
            dst_ref[:, d0:d0 + width] = (t if fn is None else fn(t)).astype(dst_ref.dtype)
        return run

    def merge_tile(y_ref, branch, gate_ref, c0, first):
        def run():
            u = _mm(y_ref[...], wbr_ref[branch, :, c0:c0 + MXU_TILE])
            g = gate_ref[:, c0:c0 + MXU_TILE] * u + u
            if first:
                merged_scr[:, c0:c0 + MXU_TILE] = g
            else:
                merged_scr[:, c0:c0 + MXU_TILE] += g
        return run

    def drain(tasks, count):
        for _ in range(min(count, len(tasks))):
            tasks.pop(0)()

    x = x_ref[0]
    ms = jnp.mean(x * x, axis=-1, keepdims=True)
    hb_scr[...] = (x * lax.rsqrt(ms + EPS) * ng_ref[...]).astype(BF16)
    hb = hb_scr[...]

    lane = lax.broadcasted_iota(jnp.int32, (ATTN_BLOCK, LANES), 1)
    low_half = lane < LANES // 2

    cos_t = cos_ref[0]
    sin_t = sin_ref[0]
    gmat = gmat_ref[...]
    akv = _mm(hb, wkv_ref[...])
    ak = akv[:, :LANES]
    kn = ak * lax.rsqrt(_mm((ak * ak).astype(BF16), gmat) + EPS) * kgain_ref[...]
    kr = kn * cos_t + pltpu.roll(kn, LANES // 2, 1) * sin_t
    k_scr[ATTN_BLOCK:ATTN_BLOCK + ts, :] = kr.astype(BF16)
    v_scr[ATTN_BLOCK:ATTN_BLOCK + ts, :] = akv[:, LANES:].astype(BF16)

    aq = _mm(hb, wq_ref[...])
    lane_ts = lax.broadcasted_iota(jnp.int32, (ts, LANES), 1)
    group0 = (lane_ts // 32) % 2 == 0
    qgain = qgain_ref[...] * (ATTN_HEAD_DIM ** -0.5)
    for c in range(4):
        qc = aq[:, c * LANES:(c + 1) * LANES]
        qn = qc * lax.rsqrt(_mm((qc * qc).astype(BF16), gmat) + EPS) * qgain
        qr = qn * cos_t + pltpu.roll(qn, LANES // 2, 1) * sin_t
        q_scr[0, c * ts:(c + 1) * ts, :] = jnp.where(group0, qr, 0.0).astype(BF16)
        q_scr[1, c * ts:(c + 1) * ts, :] = jnp.where(group0, 0.0, qr).astype(BF16)

    tasks = [proj_tile(wsm_ref, 0, psm_scr, 0, width=LANES)]
    tasks += [proj_tile(wgqk_ref, t * MXU_TILE, gqk_scr, t * MXU_TILE) for t in range(2)]
    tasks += [proj_tile(wgv_ref, t * MXU_TILE, gv_scr, t * MXU_TILE) for t in range(2)]
    tasks += [proj_tile(wga_ref, t * MXU_TILE, sga_scr, t * MXU_TILE, _silu_of_double) for t in range(2)]
    tasks += [proj_tile(wm_ref, t * MXU_TILE, gate_ac_scr, t * MXU_TILE, jnp.tanh) for t in range(4)]
    per_blk = -(-len(tasks) // nblk)

    qi = lax.broadcasted_iota(jnp.int32, (ATTN_BLOCK, 2 * ATTN_BLOCK), 0)
    ki = lax.broadcasted_iota(jnp.int32, (ATTN_BLOCK, 2 * ATTN_BLOCK), 1)
    in_band = (ki > qi) & (ki <= qi + ATTN_BLOCK)
    bias = jnp.where(in_band, 0.0, MASK_VALUE)
    bias_first = jnp.where(in_band & ((ki >= ATTN_BLOCK) | (j > 0)), 0.0, MASK_VALUE)
    for n in range(nblk):
        kb = k_scr[n * ATTN_BLOCK:(n + 2) * ATTN_BLOCK, :]
        vb = v_scr[n * ATTN_BLOCK:(n + 2) * ATTN_BLOCK, :]
        blk_bias = bias_first if n == 0 else bias
        outs = []
        for g in range(ATTN_KV_HEADS):
            lhs = jnp.concatenate(
                [q_scr[g, c * ts + n * ATTN_BLOCK:c * ts + (n + 1) * ATTN_BLOCK, :] for c in range(4)],
                axis=0)
            s_all = _mm_nt(lhs, kb)
            probs, scales = [], []
            for r in range(4 * ATTN_BLOCK // SOFTMAX_ROWS):
                r0 = r * SOFTMAX_ROWS
                q0 = r0 % ATTN_BLOCK
                sink = sinks_ref[4 * g + r0 // ATTN_BLOCK]
                s = s_all[r0:r0 + SOFTMAX_ROWS] + blk_bias[q0:q0 + SOFTMAX_ROWS]
                m = jnp.maximum(jnp.max(s, axis=1, keepdims=True), sink)
                p = jnp.exp(s - m)
                denom = jnp.sum(p, axis=1, keepdims=True) + jnp.exp(sink - m)
                probs.append(p.astype(BF16))
                scales.append(1.0 / denom)
            outs.append(_mm(jnp.concatenate(probs, axis=0), vb) * jnp.concatenate(scales, axis=0))
            drain(tasks, (per_blk + 1 - g) // 2)
        for c in range(4):
            bra_scr[n * ATTN_BLOCK:(n + 1) * ATTN_BLOCK, c * LANES:(c + 1) * LANES] = jnp.where(
                low_half,
                outs[0][c * ATTN_BLOCK:(c + 1) * ATTN_BLOCK],
                outs[1][c * ATTN_BLOCK:(c + 1) * ATTN_BLOCK])
    drain(tasks, len(tasks))
    k_scr[0:ATTN_BLOCK, :] = k_scr[ts:ts + ATTN_BLOCK, :]
    v_scr[0:ATTN_BLOCK, :] = v_scr[ts:ts + ATTN_BLOCK, :]
    ya_scr[...] = (bra_scr[...] * sga_scr[...]).astype(BF16)

    psm = psm_scr[...]
    logit = _mm(psm.astype(BF16), wup_ref[...]) + bgate_ref[...]
    log_alpha = (jnp.minimum(logit, 0.0) - jnp.log(1.0 + jnp.exp(-jnp.abs(logit)))) * (
        1.0 / GLA_GATE_NORMALIZER)
    dt = _softplus(psm + dtb_ref[...])
    adt = dt * (-jnp.exp(alog_ref[...]))
    cs_in = jnp.concatenate([log_alpha, adt], axis=1)
    ltri = ltri_ref[...]
    cs_rows = ltri.shape[0]
    cs_parts = []
    for r in range(ts // cs_rows):
        pieces = _split3(cs_in[r * cs_rows:(r + 1) * cs_rows])
        cs_parts.append(_mm(ltri, pieces[0]) + _mm(ltri, pieces[1]) + _mm(ltri, pieces[2]))
    cs = cs_parts[0] if len(cs_parts) == 1 else jnp.concatenate(cs_parts, axis=0)
    bcum = cs[:, :GLA_HEADS * GLA_DK]
    acs = cs[:, GLA_HEADS * GLA_DK:]

    tasks = [proj_tile(wxbc_ref, t * MXU_TILE, conv_scr.at[CONV_PAD:CONV_PAD + ts], t * MXU_TILE)
             for t in range(4)]
    tasks += [merge_tile(ya_scr, 0, gate_ac_scr, t * MXU_TILE, True) for t in range(4)]
    tasks += [proj_tile(wgg_ref, t * MXU_TILE, sgg_scr, t * MXU_TILE, _silu_of_double) for t in range(2)]
    tasks += [proj_tile(wm_ref, D_MODEL + t * MXU_TILE, gate_b_scr, t * MXU_TILE, jnp.tanh)
              for t in range(4)]
    tasks += [proj_tile(wz_ref, t * MXU_TILE, sz_scr, t * MXU_TILE, _silu_of_double) for t in range(2)]
    per_chunk = -(-len(tasks) // nchunk)

    dkw = GLA_HEADS * GLA_DK
    lane_k = lax.broadcasted_iota(jnp.int32, (CHUNK, dkw), 1) // GLA_DK
    ci = lax.broadcasted_iota(jnp.int32, (GLA_HEADS * CHUNK, CHUNK), 0) % CHUNK
    cj = lax.broadcasted_iota(jnp.int32, (GLA_HEADS * CHUNK, CHUNK), 1)
    causal4 = ci >= cj
    st_blockdiag = (lax.broadcasted_iota(jnp.int32, gla_scr.shape, 0) // GLA_DV
                    == lax.broadcasted_iota(jnp.int32, gla_scr.shape, 1) // GLA_DK)
    gla_state = gla_scr[...]
    for c in range(nchunk):
        rs = slice(c * CHUNK, (c + 1) * CHUNK)
        bc = bcum[rs]
        bmid = bc[CHUNK // 2 - 1:CHUNK // 2]
        blast = bc[CHUNK - 1:CHUNK]
        q = gqk_scr[rs, :dkw] * (GLA_DK ** -0.5)
        k = gqk_scr[rs, dkw:]
        v = gv_scr[rs, :]
        q_dec = (q * jnp.exp(bc)).astype(BF16)
        q_mid = q * jnp.exp(bc - bmid)
        k_mid = (k * jnp.exp(bmid - bc)).astype(BF16)
        k_end = (k * jnp.exp(blast - bc)).astype(BF16)
        lhs = jnp.concatenate(
            [jnp.where(lane_k == h, q_mid, 0.0).astype(BF16) for h in range(GLA_HEADS)], axis=0)
        attn = jnp.where(causal4, _mm_nt(lhs, k_mid), 0.0).astype(BF16)
        o_inter = _mm_nt(q_dec, gla_state.astype(BF16))
        drain(tasks, per_chunk // 2)
        for h in range(GLA_HEADS):
            hs = slice(h * GLA_DV, (h + 1) * GLA_DV)
            o = _mm(attn[h * CHUNK:(h + 1) * CHUNK], v[:, hs]) + o_inter[:, hs]
            brb_scr[rs, hs] = o * lax.rsqrt(jnp.mean(o * o, axis=-1, keepdims=True) + EPS)
        gla_state = jnp.where(st_blockdiag, gla_state * jnp.exp(blast) + _mm_tn(v, k_end), 0.0)
        drain(tasks, per_chunk - per_chunk // 2)
    drain(tasks, len(tasks))
    gla_scr[...] = gla_state
    yb_scr[...] = (brb_scr[...] * gnorm_ref[...] * sgg_scr[...]).astype(BF16)

    tasks = [proj_tile(wm_ref, 2 * D_MODEL + t * MXU_TILE, gate_ac_scr, t * MXU_TILE, jnp.tanh)
             for t in range(4)]
    tasks += [merge_tile(yb_scr, 1, gate_b_scr, t * MXU_TILE, False) for t in range(4)]
    per_chunk = -(-len(tasks) // nchunk)

    nstate = SSD_GROUPS * SSD_D_STATE
    emat = emat_ref[...]
    tril = (lax.broadcasted_iota(jnp.int32, (CHUNK, CHUNK), 0)
            >= lax.broadcasted_iota(jnp.int32, (CHUNK, CHUNK), 1))
    gw = SSD_D_INNER // SSD_GROUPS
    ssd_state = ssd_scr[...]
    half_convw = 0.5 * convw_ref[...]
    half_convb = 0.5 * convb_ref[...]
    for c in range(nchunk):
        rs = slice(c * CHUNK, (c + 1) * CHUNK)
        r0 = CONV_PAD + c * CHUNK
        acc = half_convb + half_convw[SSD_CONV - 1:SSD_CONV, :] * conv_scr[r0:r0 + CHUNK, :]
        for i in range(SSD_CONV - 1):
            off = r0 - (SSD_CONV - 1) + i
            acc = acc + half_convw[i:i + 1, :] * conv_scr[off:off + CHUNK, :]
        xbc = _silu_of_double(acc)
        xs = xbc[:, :SSD_D_INNER]
        bmat = xbc[:, SSD_D_INNER:SSD_D_INNER + nstate].astype(BF16)
        cmat = xbc[:, SSD_D_INNER + nstate:]
        a = acs[rs]
        a_last = a[CHUNK - 1:CHUNK]
        w = dt[rs] * jnp.exp(a_last - a)
        w_hi = w.astype(BF16)
        w_lo = (w - w_hi.astype(F32)).astype(BF16)
        xw = (xs * (_mm(w_hi, emat) + _mm(w_lo, emat))).astype(BF16)
        al3 = _split3(jnp.broadcast_to(a_last, (8, LANES)))
        chunk_decay = jnp.exp(_mm(al3[0], emat) + _mm(al3[1], emat) + _mm(al3[2], emat))[0:1]
        a_t = a.T
        dt_t = dt[rs].T
        x_c = xs.astype(BF16)
        state_b = ssd_state.astype(BF16)
        drain(tasks, per_chunk // 2)
        ys, new_state = [], []
        for g in range(SSD_GROUPS):
            gs = slice(g * SSD_D_STATE, (g + 1) * SSD_D_STATE)
            b_g = bmat[:, gs]
            c_g = cmat[:, gs]
            cb = _mm_nt(c_g.astype(BF16), b_g)
            for hh in range(SSD_HEADS // SSD_GROUPS):
                h = g * (SSD_HEADS // SSD_GROUPS) + hh
                a_col = a[:, h:h + 1]
                decay = jnp.where(tril, jnp.exp(a_col - a_t[h:h + 1, :]), 0.0)
                m_h = cb * decay * dt_t[h:h + 1, :]
                lhs = jnp.concatenate([m_h, c_g * jnp.exp(a_col)], axis=1).astype(BF16)
                ps = slice((h // 2) * LANES, (h // 2 + 1) * LANES)
                rhs = jnp.concatenate([x_c[:, ps], state_b[:, ps]], axis=0)
                ys.append(_mm(lhs, rhs))
            ws = slice(g * gw, (g + 1) * gw)
            new_state.append(ssd_state[:, ws] * chunk_decay[:, ws] + _mm_tn(b_g, xw[:, ws]))
        ssd_state = jnp.concatenate(new_state, axis=1)
        y = jnp.concatenate(
            [jnp.where(low_half, ys[2 * p], ys[2 * p + 1]) for p in range(SSD_HEADS // 2)], axis=1)
        u = (y + xs * dexp_ref[...]) * sz_scr[rs, :]
        normed = []
        for g in range(SSD_GROUPS):
            ug = u[:, g * gw:(g + 1) * gw]
            normed.append(ug * lax.rsqrt(jnp.mean(ug * ug, axis=-1, keepdims=True) + EPS))
        brc_scr[rs, :] = (jnp.concatenate(normed, axis=1) * snorm_ref[...]).astype(BF16)
        drain(tasks, per_chunk - per_chunk // 2)
    drain(tasks, len(tasks))
    ssd_scr[...] = ssd_state
    conv_scr[0:CONV_PAD, :] = conv_scr[ts:ts + CONV_PAD, :]

    u_c = _mm(brc_scr[...], wbr_ref[2])
    merged = merged_scr[...] + (gate_ac_scr[...] * u_c + u_c)
    out_ref[0] = x_ref[0] + _mm(merged.astype(BF16), wo_ref[...])


def _to_q_lanes(w):
    lead = w.shape[:-1]
    w = w.reshape(*lead, ATTN_KV_HEADS, 4, 2, 32)
    return jnp.moveaxis(w, -4, -2).reshape(*lead, ATTN_HEADS * ATTN_HEAD_DIM)


def _to_k_lanes(w):
    lead = w.shape[:-1]
    w = w.reshape(*lead, ATTN_KV_HEADS, 2, 32)
    return jnp.swapaxes(w, -3, -2).reshape(*lead, ATTN_KV_HEADS * ATTN_HEAD_DIM)


def _to_ya_lanes(w, axis):
    shape = w.shape
    w = w.reshape(*shape[:axis], ATTN_KV_HEADS, 4, ATTN_HEAD_DIM, *shape[axis + 1:])
    return jnp.swapaxes(w, axis, axis + 1).reshape(shape)


def _const_tables():
    lane = np.arange(LANES)
    gmat = ((lane[:, None] // 32) % 2 == (lane[None, :] // 32) % 2).astype(np.float32) / ATTN_HEAD_DIM
    r = np.arange(2 * CHUNK)
    ltri = ((r[:, None] // CHUNK == r[None, :] // CHUNK) & (r[:, None] >= r[None, :])).astype(np.float32)
    emat = np.zeros((LANES, SSD_D_INNER), np.float32)
    for h in range(SSD_HEADS):
        emat[h, h * SSD_HEAD_DIM:(h + 1) * SSD_HEAD_DIM] = 1.0
    return (jnp.asarray(gmat, BF16), jnp.asarray(ltri, BF16), jnp.asarray(emat, BF16))


def _row(v):
    return v.reshape(1, -1).astype(F32)


def _prep_layer(w_in, attn_q_norm, attn_k_norm, gla_w_gate_up, gla_b_gate, gla_out_norm,
                ssd_conv_w, ssd_conv_b, ssd_dt_bias, ssd_A_log, ssd_D, ssd_out_norm, w_branch, w_out):
    c = IN_PROJ_CUTS
    seg = lambda i: w_in[:, c[i]:c[i + 1]]
    d_in = w_in.shape[0]
    wq = _to_q_lanes(seg(0))
    wkv = jnp.concatenate([_to_k_lanes(seg(1)), seg(2)], axis=1)
    wga = _to_ya_lanes(seg(3), 1)
    wgqk = jnp.concatenate([seg(4), seg(5)], axis=1)
    wgv, wgg = seg(6), seg(7)
    wsm = jnp.concatenate([seg(10), seg(8), jnp.zeros((d_in, LANES - 24), w_in.dtype)], axis=1)
    wxbc, wz, wm = seg(9), seg(11), seg(12)
    wbr = jnp.concatenate([_to_ya_lanes(w_branch[0:1], 1), w_branch[1:]], axis=0)
    lane_gain = lambda g: jnp.repeat(g.reshape(2, 1, 32), 2, axis=1).reshape(1, LANES).astype(F32)
    wup = jnp.zeros((LANES, gla_w_gate_up.shape[1]), F32).at[8:8 + GLA_GATE_RANK].set(gla_w_gate_up)
    pad8 = lambda v: jnp.zeros((1, LANES), F32).at[0, :SSD_HEADS].set(v)
    wga, wgg, wz, wm, wbr = (0.5 * w for w in (wga, wgg, wz, wm, wbr))
    big = [w.astype(BF16) for w in (wq, wkv, wga, wgqk, wgv, wgg, wsm, wxbc, wz, wm, wbr, w_out)]
    small = dict(
        qgain=lane_gain(attn_q_norm), kgain=lane_gain(attn_k_norm),
        wup=wup.astype(BF16), bgate=_row(gla_b_gate), gnorm=_row(jnp.tile(gla_out_norm, GLA_HEADS)),
        convw=ssd_conv_w.astype(F32), convb=_row(ssd_conv_b), dtb=pad8(ssd_dt_bias),
        alog=pad8(ssd_A_log), dexp=_row(jnp.repeat(ssd_D, SSD_HEAD_DIM)), snorm=_row(ssd_out_norm))
    return big, small


def _const_spec(shape):
    nd = len(shape)
    return pl.BlockSpec(shape, lambda b, j, _nd=nd: (0,) * _nd, pipeline_mode=pl.Buffered(1))


def _layer_call(x, cos_t, sin_t, sinks, norm_g, big, small, consts, ts):
    bsz, seq, d = x.shape
    gmat, ltri, emat = consts
    operands = [sinks.astype(F32), x, cos_t, sin_t, _row(norm_g), *big,
                small["qgain"], small["kgain"], gmat, small["wup"], small["bgate"], small["gnorm"],
                ltri, emat, small["convw"], small["convb"], small["dtb"], small["alog"],
                small["dexp"], small["snorm"]]
    tok = lambda w: pl.BlockSpec((1, ts, w), lambda b, j: (b, j, 0))
    in_specs = [pl.BlockSpec(memory_space=pltpu.SMEM), tok(d), tok(LANES), tok(LANES)]
    in_specs += [_const_spec(op.shape) for op in operands[4:]]
    conv_dim = small["convw"].shape[1]
    scratch = [
        pltpu.VMEM((ts, d), BF16),
        pltpu.VMEM((ATTN_KV_HEADS, 4 * ts, LANES), BF16),
        pltpu.VMEM((ts + ATTN_BLOCK, LANES), BF16),
        pltpu.VMEM((ts + ATTN_BLOCK, LANES), BF16),
        pltpu.VMEM((ts, 512), F32),
        pltpu.VMEM((ts, 512), F32),
        pltpu.VMEM((ts, 512), BF16),
        pltpu.VMEM((ts, d), F32),
        pltpu.VMEM((ts + CONV_PAD, conv_dim), F32),
        pltpu.VMEM((GLA_HEADS * GLA_DV, GLA_HEADS * GLA_DK), F32),
        pltpu.VMEM((SSD_D_STATE, SSD_D_INNER), F32),
        pltpu.VMEM((ts, 512), F32),
        pltpu.VMEM((ts, 512), BF16),
        pltpu.VMEM((ts, LANES), F32),
        pltpu.VMEM((ts, 512), F32),
        pltpu.VMEM((ts, 512), F32),
        pltpu.VMEM((ts, 512), F32),
        pltpu.VMEM((ts, d), F32),
        pltpu.VMEM((ts, d), F32),
        pltpu.VMEM((ts, 512), BF16),
        pltpu.VMEM((ts, 512), BF16),
    ]
    return pl.pallas_call(
        functools.partial(_layer_kernel, ts=ts),
        grid=(bsz, seq // ts),
        in_specs=in_specs,
        out_specs=pl.BlockSpec((1, ts, d), lambda b, j: (b, j, 0)),
        out_shape=jax.ShapeDtypeStruct(x.shape, x.dtype),
        scratch_shapes=scratch,
        compiler_params=pltpu.CompilerParams(
            dimension_semantics=("arbitrary", "arbitrary"),
            vmem_limit_bytes=VMEM_LIMIT_BYTES),
        name="hybrid_layer",
    )(*operands)


def _rope_tables(positions):
    bsz, seq = positions.shape
    inv_freq = ROPE_THETA ** (-jnp.arange(0, ATTN_HEAD_DIM, 2, dtype=F32) / ATTN_HEAD_DIM)
    freq = jnp.tile(inv_freq, LANES // inv_freq.shape[0]).reshape(1, LANES)
    spec = pl.BlockSpec((1, seq, LANES), lambda b: (b, 0, 0))
    return pl.pallas_call(
        _rope_table_kernel,
        grid=(bsz,),
        in_specs=[pl.BlockSpec((1, seq, 1), lambda b: (b, 0, 0)),
                  pl.BlockSpec((1, LANES), lambda b: (0, 0))],
        out_specs=[spec, spec],
        out_shape=[jax.ShapeDtypeStruct((bsz, seq, LANES), F32)] * 2,
        name="rope_tables",
    )(positions.reshape(bsz, seq, 1), freq)


def kernel(x, positions, norm_g, w_in, attn_q_norm, attn_k_norm, attn_sinks, gla_w_gate_up, gla_b_gate,
           gla_out_norm, ssd_conv_w, ssd_conv_b, ssd_dt_bias, ssd_A_log, ssd_D, ssd_out_norm,
           w_branch, w_out):
    seq = x.shape[1]
    ts = min(SEQ_BLOCK, seq)
    assert seq % ts == 0 and ts % (2 * CHUNK) == 0 and x.shape[2] == D_MODEL
    cos_t, sin_t = _rope_tables(positions)
    consts = _const_tables()
    for i in range(w_in.shape[0]):
        big, small = _prep_layer(
            w_in[i], attn_q_norm[i], attn_k_norm[i], gla_w_gate_up[i], gla_b_gate[i], gla_out_norm[i],
            ssd_conv_w[i], ssd_conv_b[i], ssd_dt_bias[i], ssd_A_log[i], ssd_D[i], ssd_out_norm[i],
            w_branch[i], w_out[i])
        x = _layer_call(x, cos_t, sin_t, attn_sinks[i], norm_g[i], big, small, consts, ts)
    return x
```

```python
import functools

import numpy as np
import jax
import jax.numpy as jnp
from jax import lax
from jax.experimental import pallas as pl
from jax.experimental.pallas import tpu as pltpu

F32 = jnp.float32
BF16 = jnp.bfloat16

D_MODEL = 1024
EPS = 1e-6
ROPE_THETA = 10000.0

ATTN_HEADS = 8
ATTN_KV_HEADS = 2
ATTN_HEAD_DIM = 64
ATTN_BLOCK = 128
GLA_HEADS = 4
GLA_DK = 64
GLA_DV = 128
GLA_GATE_RANK = 16
GLA_GATE_NORMALIZER = 16.0
SSD_HEADS = 8
SSD_HEAD_DIM = 64
SSD_GROUPS = 2
SSD_D_STATE = 128
SSD_D_INNER = 512
SSD_CONV = 4

IN_PROJ_SIZES = (512, 128, 128, 512, 256, 256, 512, 512, 16, 1024, 8, 512, 3072)
IN_PROJ_CUTS = tuple(int(v) for v in np.cumsum((0,) + IN_PROJ_SIZES))

LANES = 128
CHUNK = 128
CONV_PAD = 8
SEQ_BLOCK = 512
VMEM_LIMIT_BYTES = 56 * 1024 * 1024
MASK_VALUE = -1e30
MXU_TILE = 256
SOFTMAX_ROWS = 64
CONV_STRIDE = 4
CONV_GROUP = 8 * CONV_STRIDE

W_Q, W_KV, W_GA, W_GQK, W_GV, W_GG, W_SM, W_XBC, W_Z, W_M, W_END = (
    0, 512, 768, 1280, 1792, 2304, 2816, 2944, 3968, 4480, 7552)
P_NORM, P_CONVB, P_CONVW = (0, 0, 1024), (1, 0, 1024), (2, 0, 1024)
P_QGAIN, P_KGAIN, P_DTB, P_ALOG, P_BGATE = (6, 0, 128), (6, 128, 128), (6, 256, 128), (6, 384, 128), (6, 512, 256)
P_GNORM, P_DEXP = (7, 0, 512), (7, 512, 512)
P_SNORM = (8, 0, 512)
P_ROWS = 16


def _mm(a, b):
    return jnp.dot(a, b, preferred_element_type=F32)


def _mm_nt(a, b):
    return lax.dot_general(a, b, (((1,), (1,)), ((), ())), preferred_element_type=F32)


def _mm_tn(a, b):
    return lax.dot_general(a, b, (((0,), (0,)), ((), ())), preferred_element_type=F32)


def _silu_of_double(vh):
    return vh * jnp.tanh(vh) + vh


def _softplus(v):
    return jnp.maximum(v, 0.0) + jnp.log(1.0 + jnp.exp(-jnp.abs(v)))


def _split3(v):
    hi = v.astype(BF16)
    r1 = v - hi.astype(F32)
    mid = r1.astype(BF16)
    lo = (r1 - mid.astype(F32)).astype(BF16)
    return hi, mid, lo


def _rope_table_kernel(pos_ref, freq_ref, cos_ref, sin_ref):
    ang = pos_ref[0].astype(F32) * freq_ref[...]
    cos_ref[0] = jnp.cos(ang)
    sin_ref[0] = jnp.sin(ang)


def _layer_kernel(
        sinks_ref, x_ref, cos_ref, sin_ref, w_ref, wbr_ref, wo_ref, wup_ref, prm_ref,
        gmat_ref, ltri_ref, emat_ref,
        out_ref,
        hb_scr, q_scr, k_scr, v_scr, bra_scr, brb_scr, brc_scr, merged_scr, conv_scr, xbc_scr,
        gla_scr, ssd_scr, gqk_scr, gv_scr, psm_scr, sga_scr, sgg_scr, sz_scr, gate_ac_scr, gate_b_scr,
        ya_scr, yb_scr,
        *, ts, layer):
    nblk = ts // ATTN_BLOCK
    nchunk = ts // CHUNK
    j = pl.program_id(1)

    @pl.when(j == 0)
    def _reset_carries():
        k_scr[0:ATTN_BLOCK, :] = jnp.zeros((ATTN_BLOCK, LANES), BF16)
        v_scr[0:ATTN_BLOCK, :] = jnp.zeros((ATTN_BLOCK, LANES), BF16)
        conv_scr[:, 0:CONV_PAD, :] = jnp.zeros((conv_scr.shape[0], CONV_PAD, LANES), F32)
        gla_scr[...] = jnp.zeros(gla_scr.shape, F32)
        ssd_scr[...] = jnp.zeros(ssd_scr.shape, F32)

    def prm(p):
        return prm_ref[p[0]:p[0] + 1, p[1]:p[1] + p[2]]

    def proj_tile(c0, dst_ref, d0, fn=None, width=MXU_TILE):
        def run():
            t = _mm(hb_scr[...], w_ref[:, c0:c0 + width])
            dst_ref[:, d0:d0 + width] = (t if fn is None else fn(t)).astype(dst_ref.dtype)
        return run

    def conv_in_tile(c0):
        def run():
            t = _mm(hb_scr[...], w_ref[:, W_XBC + c0:W_XBC + c0 + MXU_TILE])
            for i in range(MXU_TILE // LANES):
                conv_scr[c0 // LANES + i, CONV_PAD:CONV_PAD + ts, :] = t[:, i * LANES:(i + 1) * LANES]
        return run

    def merge_tile(y_ref, branch, gate_ref, c0, first):
        def run():
            u = _mm(y_ref[...], wbr_ref[branch, :, c0:c0 + MXU_TILE])
            g = gate_ref[:, c0:c0 + MXU_TILE] * u + u
            if first:
                merged_scr[:, c0:c0 + MXU_TILE] = g
            else:
                merged_scr[:, c0:c0 + MXU_TILE] += g
        return run

    def drain(tasks, count):
        for _ in range(min(count, len(tasks))):
            tasks.pop(0)()

    x = x_ref[0]
    ms = jnp.mean(x * x, axis=-1, keepdims=True)
    hb_scr[...] = (x * lax.rsqrt(ms + EPS) * prm(P_NORM)).astype(BF16)
    hb = hb_scr[...]

    lane = lax.broadcasted_iota(jnp.int32, (ATTN_BLOCK, LANES), 1)
    low_half = lane < LANES // 2

    cos_t = cos_ref[0]
    sin_t = sin_ref[0]
    gmat = gmat_ref[...]
    akv = _mm(hb, w_ref[:, W_KV:W_GA])
    ak = akv[:, :LANES]
    kn = ak * lax.rsqrt(_mm((ak * ak).astype(BF16), gmat) + EPS) * prm(P_KGAIN)
    kr = kn * cos_t + pltpu.roll(kn, LANES // 2, 1) * sin_t
    k_scr[ATTN_BLOCK:ATTN_BLOCK + ts, :] = kr.astype(BF16)
    v_scr[ATTN_BLOCK:ATTN_BLOCK + ts, :] = akv[:, LANES:].astype(BF16)

    aq = _mm(hb, w_ref[:, W_Q:W_KV])
    lane_ts = lax.broadcasted_iota(jnp.int32, (ts, LANES), 1)
    group0 = (lane_ts // 32) % 2 == 0
    qgain = prm(P_QGAIN) * (ATTN_HEAD_DIM ** -0.5)
    for c in range(4):
        qc = aq[:, c * LANES:(c + 1) * LANES]
        qn = qc * lax.rsqrt(_mm((qc * qc).astype(BF16), gmat) + EPS) * qgain
        qr = qn * cos_t + pltpu.roll(qn, LANES // 2, 1) * sin_t
        q_scr[0, c * ts:(c + 1) * ts, :] = jnp.where(group0, qr, 0.0).astype(BF16)
        q_scr[1, c * ts:(c + 1) * ts, :] = jnp.where(group0, 0.0, qr).astype(BF16)

    tasks = [proj_tile(W_SM, psm_scr, 0, width=LANES)]
    tasks += [proj_tile(W_GQK + t * MXU_TILE, gqk_scr, t * MXU_TILE) for t in range(2)]
    tasks += [proj_tile(W_GV + t * MXU_TILE, gv_scr, t * MXU_TILE) for t in range(2)]
    tasks += [conv_in_tile(t * MXU_TILE) for t in range(4)]
    drain(tasks, len(tasks))
    tasks = [proj_tile(W_GA + t * MXU_TILE, sga_scr, t * MXU_TILE, _silu_of_double) for t in range(2)]
    tasks += [proj_tile(W_GG + t * MXU_TILE, sgg_scr, t * MXU_TILE, _silu_of_double) for t in range(2)]
    tasks += [proj_tile(W_Z + t * MXU_TILE, sz_scr, t * MXU_TILE, _silu_of_double) for t in range(2)]
    tasks += [proj_tile(W_M + t * MXU_TILE, gate_ac_scr, t * MXU_TILE, jnp.tanh) for t in range(4)]
    per_blk = -(-len(tasks) // nblk)

    qi = lax.broadcasted_iota(jnp.int32, (ATTN_BLOCK, ATTN_BLOCK), 0)
    ki = lax.broadcasted_iota(jnp.int32, (ATTN_BLOCK, ATTN_BLOCK), 1)
    from_prev = ki > qi
    first_block_bias = jnp.where(j > 0, 0.0, MASK_VALUE)
    for n in range(nblk):
        kb = k_scr[n * ATTN_BLOCK:(n + 2) * ATTN_BLOCK, :]
        vb = v_scr[n * ATTN_BLOCK:(n + 2) * ATTN_BLOCK, :]
        outs = []
        for g in range(ATTN_KV_HEADS):
            lhs = jnp.concatenate(
                [q_scr[g, c * ts + n * ATTN_BLOCK:c * ts + (n + 1) * ATTN_BLOCK, :] for c in range(4)],
                axis=0)
            s_all = _mm_nt(lhs, kb)
            probs, scales = [], []
            for r in range(4 * ATTN_BLOCK // SOFTMAX_ROWS):
                r0 = r * SOFTMAX_ROWS
                q0 = r0 % ATTN_BLOCK
                sink = sinks_ref[layer, 4 * g + r0 // ATTN_BLOCK]
                s_prev = s_all[r0:r0 + SOFTMAX_ROWS, :ATTN_BLOCK]
                if n == 0:
                    s_prev = s_prev + first_block_bias
                prev_sel = from_prev[q0:q0 + SOFTMAX_ROWS]
                s = jnp.where(prev_sel, s_prev, s_all[r0:r0 + SOFTMAX_ROWS, ATTN_BLOCK:])
                m = jnp.maximum(jnp.max(s, axis=1, keepdims=True), sink)
                p = jnp.exp(s - m)
                denom = jnp.sum(p, axis=1, keepdims=True) + jnp.exp(sink - m)
                pb = p.astype(BF16)
                zero = jnp.zeros_like(pb)
                probs.append(jnp.concatenate(
                    [jnp.where(prev_sel, pb, zero), jnp.where(prev_sel, zero, pb)], axis=1))
                scales.append(1.0 / denom)
            outs.append(_mm(jnp.concatenate(probs, axis=0), vb) * jnp.concatenate(scales, axis=0))
            drain(tasks, (per_blk + 1 - g) // 2)
        for c in range(4):
            bra_scr[n * ATTN_BLOCK:(n + 1) * ATTN_BLOCK, c * LANES:(c + 1) * LANES] = jnp.where(
                low_half,
                outs[0][c * ATTN_BLOCK:(c + 1) * ATTN_BLOCK],
                outs[1][c * ATTN_BLOCK:(c + 1) * ATTN_BLOCK])
    drain(tasks, len(tasks))
    k_scr[0:ATTN_BLOCK, :] = k_scr[ts:ts + ATTN_BLOCK, :]
    v_scr[0:ATTN_BLOCK, :] = v_scr[ts:ts + ATTN_BLOCK, :]
    ya_scr[...] = (bra_scr[...] * sga_scr[...]).astype(BF16)

    psm = psm_scr[...]
    logit = _mm(psm.astype(BF16), wup_ref[...]) + prm(P_BGATE)
    log_alpha = (jnp.minimum(logit, 0.0) - jnp.log(1.0 + jnp.exp(-jnp.abs(logit)))) * (
        1.0 / GLA_GATE_NORMALIZER)
    dt = _softplus(psm + prm(P_DTB))
    adt = dt * (-jnp.exp(prm(P_ALOG)))
    cs_in = jnp.concatenate([log_alpha, adt], axis=1)
    ltri = ltri_ref[...]
    cs_rows = ltri.shape[0]
    cs_parts = []
    for r in range(ts // cs_rows):
        pieces = _split3(cs_in[r * cs_rows:(r + 1) * cs_rows])
        cs_parts.append(_mm(ltri, pieces[0]) + _mm(ltri, pieces[1]) + _mm(ltri, pieces[2]))
    cs = cs_parts[0] if len(cs_parts) == 1 else jnp.concatenate(cs_parts, axis=0)
    bcum = cs[:, :GLA_HEADS * GLA_DK]
    acs = cs[:, GLA_HEADS * GLA_DK:]

    tasks = [merge_tile(ya_scr, 0, gate_ac_scr, t * MXU_TILE, True) for t in range(4)]
    tasks += [proj_tile(W_M + D_MODEL + t * MXU_TILE, gate_b_scr, t * MXU_TILE, jnp.tanh)
              for t in range(4)]
    per_chunk = -(-len(tasks) // nchunk)

    dkw = GLA_HEADS * GLA_DK
    lane_k = lax.broadcasted_iota(jnp.int32, (CHUNK, dkw), 1) // GLA_DK
    ci = lax.broadcasted_iota(jnp.int32, (GLA_HEADS * CHUNK, CHUNK), 0) % CHUNK
    cj = lax.broadcasted_iota(jnp.int32, (GLA_HEADS * CHUNK, CHUNK), 1)
    causal4 = ci >= cj
    st_blockdiag = (lax.broadcasted_iota(jnp.int32, gla_scr.shape, 0) // GLA_DV
                    == lax.broadcasted_iota(jnp.int32, gla_scr.shape, 1) // GLA_DK)
    gla_state = gla_scr[...]
    for c in range(nchunk):
        rs = slice(c * CHUNK, (c + 1) * CHUNK)
        bc = bcum[rs]
        bmid = bc[CHUNK // 2 - 1:CHUNK // 2]
        blast = bc[CHUNK - 1:CHUNK]
        q = gqk_scr[rs, :dkw] * (GLA_DK ** -0.5)
        k = gqk_scr[rs, dkw:]
        v = gv_scr[rs, :]
        q_dec = (q * jnp.exp(bc)).astype(BF16)
        q_mid = q * jnp.exp(bc - bmid)
        k_mid = (k * jnp.exp(bmid - bc)).astype(BF16)
        k_end = (k * jnp.exp(blast - bc)).astype(BF16)
        lhs = jnp.concatenate(
            [jnp.where(lane_k == h, q_mid, 0.0).astype(BF16) for h in range(GLA_HEADS)], axis=0)
        attn = jnp.where(causal4, _mm_nt(lhs, k_mid), 0.0).astype(BF16)
        o_inter = _mm_nt(q_dec, gla_state.astype(BF16))
        drain(tasks, per_chunk // 2)
        for h in range(GLA_HEADS):
            hs = slice(h * GLA_DV, (h + 1) * GLA_DV)
            o = _mm(attn[h * CHUNK:(h + 1) * CHUNK], v[:, hs]) + o_inter[:, hs]
            brb_scr[rs, hs] = o * lax.rsqrt(jnp.mean(o * o, axis=-1, keepdims=True) + EPS)
        gla_state = jnp.where(st_blockdiag, gla_state * jnp.exp(blast) + _mm_tn(v, k_end), 0.0)
        drain(tasks, per_chunk - per_chunk // 2)
    drain(tasks, len(tasks))
    gla_scr[...] = gla_state
    yb_scr[...] = (brb_scr[...] * prm(P_GNORM) * sgg_scr[...]).astype(BF16)

    tasks = [merge_tile(yb_scr, 1, gate_b_scr, t * MXU_TILE, False) for t in range(4)]
    tasks += [proj_tile(W_M + 2 * D_MODEL + t * MXU_TILE, gate_ac_scr, t * MXU_TILE, jnp.tanh)
              for t in range(4)]
    per_chunk = -(-len(tasks) // nchunk)

    emat = emat_ref[...]
    tril = (lax.broadcasted_iota(jnp.int32, (CHUNK, CHUNK), 0)
            >= lax.broadcasted_iota(jnp.int32, (CHUNK, CHUNK), 1))
    gw = SSD_D_INNER // SSD_GROUPS
    ssd_state = ssd_scr[...]
    nslab = conv_scr.shape[0]
    row = lambda r, sl: jnp.broadcast_to(prm_ref[r:r + 1, sl * LANES:(sl + 1) * LANES] * 0.5, (8, LANES))
    half_convb = [row(P_CONVB[0], sl) for sl in range(nslab)]
    half_convw = [[row(P_CONVW[0] + i, sl) for i in range(SSD_CONV)] for sl in range(nslab)]
    for c in range(nchunk):
        rs = slice(c * CHUNK, (c + 1) * CHUNK)
        for sl in range(nslab):
            for grp in range(CHUNK // CONV_GROUP):
                base = CONV_PAD + c * CHUNK + grp * CONV_GROUP
                taps = [conv_scr[sl, pl.ds(base + m, 8, stride=CONV_STRIDE), :]
                        for m in range(1 - SSD_CONV, CONV_STRIDE)]
                for m in range(CONV_STRIDE):
                    acc = half_convb[sl]
                    for i in range(SSD_CONV):
                        acc = acc + half_convw[sl][i] * taps[m + i]
                    xbc_scr[sl, pl.ds(c * CHUNK + grp * CONV_GROUP + m, 8, stride=CONV_STRIDE), :] = (
                        _silu_of_double(acc))
        nx = SSD_D_INNER // LANES
        xs = jnp.concatenate([xbc_scr[sl, rs, :] for sl in range(nx)], axis=1)
        bmat = [xbc_scr[nx + g, rs, :].astype(BF16) for g in range(SSD_GROUPS)]
        cmat = [xbc_scr[nx + SSD_GROUPS + g, rs, :] for g in range(SSD_GROUPS)]
        a = acs[rs]
        a_last = a[CHUNK - 1:CHUNK]
        w = dt[rs] * jnp.exp(a_last - a)
        w_hi = w.astype(BF16)
        w_lo = (w - w_hi.astype(F32)).astype(BF16)
        xw = (xs * (_mm(w_hi, emat) + _mm(w_lo, emat))).astype(BF16)
        al3 = _split3(jnp.broadcast_to(a_last, (8, LANES)))
        chunk_decay = jnp.exp(_mm(al3[0], emat) + _mm(al3[1], emat) + _mm(al3[2], emat))[0:1]
        a_t = a.T
        dt_t = dt[rs].T
        x_c = xs.astype(BF16)
        state_b = ssd_state.astype(BF16)
        drain(tasks, per_chunk // 2)
        ys, new_state = [], []
        for g in range(SSD_GROUPS):
            b_g = bmat[g]
            c_g = cmat[g]
            cb = _mm_nt(c_g.astype(BF16), b_g)
            for hh in range(SSD_HEADS // SSD_GROUPS):
                h = g * (SSD_HEADS // SSD_GROUPS) + hh
                a_col = a[:, h:h + 1]
                decay = jnp.where(tril, jnp.exp(a_col - a_t[h:h + 1, :]), 0.0)
                m_h = cb * decay * dt_t[h:h + 1, :]
                lhs = jnp.concatenate([m_h, c_g * jnp.exp(a_col)], axis=1).astype(BF16)
                ps = slice((h // 2) * LANES, (h // 2 + 1) * LANES)
                rhs = jnp.concatenate([x_c[:, ps], state_b[:, ps]], axis=0)
                ys.append(_mm(lhs, rhs))
            ws = slice(g * gw, (g + 1) * gw)
            new_state.append(ssd_state[:, ws] * chunk_decay[:, ws] + _mm_tn(b_g, xw[:, ws]))
        ssd_state = jnp.concatenate(new_state, axis=1)
        y = jnp.concatenate(
            [jnp.where(low_half, ys[2 * p], ys[2 * p + 1]) for p in range(SSD_HEADS // 2)], axis=1)
        u = (y + xs * prm(P_DEXP)) * sz_scr[rs, :]
        normed = []
        for g in range(SSD_GROUPS):
            ug = u[:, g * gw:(g + 1) * gw]
            normed.append(ug * lax.rsqrt(jnp.mean(ug * ug, axis=-1, keepdims=True) + EPS))
        brc_scr[rs, :] = (jnp.concatenate(normed, axis=1) * prm(P_SNORM)).astype(BF16)
        drain(tasks, per_chunk - per_chunk // 2)
    drain(tasks, len(tasks))
    ssd_scr[...] = ssd_state
    conv_scr[:, 0:CONV_PAD, :] = conv_scr[:, ts:ts + CONV_PAD, :]

    u_c = _mm(brc_scr[...], wbr_ref[2])
    merged = merged_scr[...] + (gate_ac_scr[...] * u_c + u_c)
    out_ref[0] = x_ref[0] + _mm(merged.astype(BF16), wo_ref[...])


def _to_q_lanes(w):
    lead = w.shape[:-1]
    w = w.reshape(*lead, ATTN_KV_HEADS, 4, 2, 32)
    return jnp.moveaxis(w, -4, -2).reshape(*lead, ATTN_HEADS * ATTN_HEAD_DIM)


def _to_k_lanes(w):
    lead = w.shape[:-1]
    w = w.reshape(*lead, ATTN_KV_HEADS, 2, 32)
    return jnp.swapaxes(w, -3, -2).reshape(*lead, ATTN_KV_HEADS * ATTN_HEAD_DIM)


def _to_ya_lanes(w, axis):
    shape = w.shape
    w = w.reshape(*shape[:axis], ATTN_KV_HEADS, 4, ATTN_HEAD_DIM, *shape[axis + 1:])
    return jnp.swapaxes(w, axis, axis + 1).reshape(shape)


def _const_tables():
    lane = np.arange(LANES)
    gmat = ((lane[:, None] // 32) % 2 == (lane[None, :] // 32) % 2).astype(np.float32) / ATTN_HEAD_DIM
    r = np.arange(2 * CHUNK)
    ltri = ((r[:, None] // CHUNK == r[None, :] // CHUNK) & (r[:, None] >= r[None, :])).astype(np.float32)
    emat = np.zeros((LANES, SSD_D_INNER), np.float32)
    for h in range(SSD_HEADS):
        emat[h, h * SSD_HEAD_DIM:(h + 1) * SSD_HEAD_DIM] = 1.0
    return (jnp.asarray(gmat, BF16), jnp.asarray(ltri, BF16), jnp.asarray(emat, BF16))


def _prep_params(norm_g, w_in, attn_q_norm, attn_k_norm, gla_w_gate_up, gla_b_gate, gla_out_norm,
                 ssd_conv_w, ssd_conv_b, ssd_dt_bias, ssd_A_log, ssd_D, ssd_out_norm, w_branch, w_out):
    c = IN_PROJ_CUTS
    seg = lambda i: w_in[..., c[i]:c[i + 1]]
    depth, d_in = w_in.shape[0], w_in.shape[1]
    w_cat = jnp.concatenate([
        _to_q_lanes(seg(0)), _to_k_lanes(seg(1)), seg(2), 0.5 * _to_ya_lanes(seg(3), 2),
        seg(4), seg(5), seg(6), 0.5 * seg(7),
        seg(10), seg(8), jnp.zeros((depth, d_in, LANES - 24), w_in.dtype),
        seg(9), 0.5 * seg(11), 0.5 * seg(12)], axis=-1).astype(BF16)
    assert w_cat.shape[-1] == W_END
    wbr = (0.5 * jnp.concatenate([_to_ya_lanes(w_branch[:, 0:1], 2), w_branch[:, 1:]], axis=1)).astype(BF16)
    wup = jnp.zeros((depth, LANES, gla_w_gate_up.shape[-1]), F32).at[:, 8:8 + GLA_GATE_RANK].set(
        gla_w_gate_up).astype(BF16)
    lane_gain = lambda g: jnp.repeat(g.reshape(depth, 2, 1, 32), 2, axis=2).reshape(depth, LANES)
    pad = lambda v, n: jnp.pad(v.astype(F32), ((0, 0), (0, n - v.shape[-1])))
    prm = jnp.zeros((depth, P_ROWS, D_MODEL), F32)
    prm = prm.at[:, P_NORM[0]].set(norm_g)
    prm = prm.at[:, P_CONVB[0]].set(ssd_conv_b)
    prm = prm.at[:, P_CONVW[0]:P_CONVW[0] + SSD_CONV].set(ssd_conv_w)
    prm = prm.at[:, P_QGAIN[0]].set(jnp.concatenate([
        lane_gain(attn_q_norm), lane_gain(attn_k_norm), pad(ssd_dt_bias, LANES), pad(ssd_A_log, LANES),
        gla_b_gate, jnp.zeros((depth, D_MODEL - 768), F32)], axis=-1))
    prm = prm.at[:, P_GNORM[0]].set(jnp.concatenate([
        jnp.tile(gla_out_norm, (1, GLA_HEADS)), jnp.repeat(ssd_D, SSD_HEAD_DIM, axis=-1)], axis=-1))
    prm = prm.at[:, P_SNORM[0], :SSD_D_INNER].set(ssd_out_norm)
    return w_cat, wbr, w_out.astype(BF16), wup, prm


def _layer_call(x, cos_t, sin_t, sinks, params, consts, layer, ts):
    bsz, seq, d = x.shape
    tok = lambda w: pl.BlockSpec((1, ts, w), lambda b, j: (b, j, 0))

    def layer_spec(arr):
        nd = arr.ndim - 1
        return pl.BlockSpec((None,) + arr.shape[1:], lambda b, j, _nd=nd: (layer,) + (0,) * _nd,
                            pipeline_mode=pl.Buffered(1))

    def const_spec(arr):
        return pl.BlockSpec(arr.shape, lambda b, j, _nd=arr.ndim: (0,) * _nd, pipeline_mode=pl.Buffered(1))

    in_specs = [pl.BlockSpec(memory_space=pltpu.SMEM), tok(d), tok(LANES), tok(LANES)]
    in_specs += [layer_spec(a) for a in params] + [const_spec(a) for a in consts]
    conv_slabs = (SSD_D_INNER + 2 * SSD_GROUPS * SSD_D_STATE) // LANES
    scratch = [
        pltpu.VMEM((ts, d), BF16),
        pltpu.VMEM((ATTN_KV_HEADS, 4 * ts, LANES), BF16),
        pltpu.VMEM((ts + ATTN_BLOCK, LANES), BF16),
        pltpu.VMEM((ts + ATTN_BLOCK, LANES), BF16),
        pltpu.VMEM((ts, 512), F32),
        pltpu.VMEM((ts, 512), F32),
        pltpu.VMEM((ts, 512), BF16),
        pltpu.VMEM((ts, d), F32),
        pltpu.VMEM((conv_slabs, ts + CONV_PAD, LANES), F32),
        pltpu.VMEM((conv_slabs, ts, LANES), F32),
        pltpu.VMEM((GLA_HEADS * GLA_DV, GLA_HEADS * GLA_DK), F32),
        pltpu.VMEM((SSD_D_STATE, SSD_D_INNER), F32),
        pltpu.VMEM((ts, 512), F32),
        pltpu.VMEM((ts, 512), BF16),
        pltpu.VMEM((ts, LANES), F32),
        pltpu.VMEM((ts, 512), F32),
        pltpu.VMEM((ts, 512), F32),
        pltpu.VMEM((ts, 512), F32),
        pltpu.VMEM((ts, d), F32),
        pltpu.VMEM((ts, d), F32),
        pltpu.VMEM((ts, 512), BF16),
        pltpu.VMEM((ts, 512), BF16),
    ]
    return pl.pallas_call(
        functools.partial(_layer_kernel, ts=ts, layer=layer),
        grid=(bsz, seq // ts),
        in_specs=in_specs,
        out_specs=pl.BlockSpec((1, ts, d), lambda b, j: (b, j, 0)),
        out_shape=jax.ShapeDtypeStruct(x.shape, x.dtype),
        scratch_shapes=scratch,
        compiler_params=pltpu.CompilerParams(
            dimension_semantics=("arbitrary", "arbitrary"),
            vmem_limit_bytes=VMEM_LIMIT_BYTES),
        name="hybrid_layer",
    )(sinks.astype(F32), x, cos_t, sin_t, *params, *consts)


def _rope_tables(positions):
    bsz, seq = positions.shape
    nfreq = ATTN_HEAD_DIM // 2
    per_row = LANES // nfreq
    inv_freq = ROPE_THETA ** (-jnp.arange(0, ATTN_HEAD_DIM, 2, dtype=F32) / ATTN_HEAD_DIM)
    freq = jnp.tile(inv_freq, per_row).reshape(1, LANES)
    pos = jnp.repeat(positions.reshape(bsz, seq // per_row, per_row), nfreq, axis=2)
    spec = pl.BlockSpec((1, seq // per_row, LANES), lambda b: (b, 0, 0))
    cos_c, sin_c = pl.pallas_call(
        _rope_table_kernel,
        grid=(bsz,),
        in_specs=[spec, pl.BlockSpec((1, LANES), lambda b: (0, 0))],
        out_specs=[spec, spec],
        out_shape=[jax.ShapeDtypeStruct((bsz, seq // per_row, LANES), F32)] * 2,
        name="rope_tables",
    )(pos, freq)
    cos_c = cos_c.reshape(bsz, seq, nfreq)
    sin_c = sin_c.reshape(bsz, seq, nfreq)
    return (jnp.concatenate([cos_c] * 4, axis=-1),
            jnp.concatenate([-sin_c, -sin_c, sin_c, sin_c], axis=-1))


def kernel(x, positions, norm_g, w_in, attn_q_norm, attn_k_norm, attn_sinks, gla_w_gate_up, gla_b_gate,
           gla_out_norm, ssd_conv_w, ssd_conv_b, ssd_dt_bias, ssd_A_log, ssd_D, ssd_out_norm,
           w_branch, w_out):
    seq = x.shape[1]
    ts = min(SEQ_BLOCK, seq)
    assert seq % ts == 0 and ts % (2 * CHUNK) == 0 and x.shape[2] == D_MODEL
    cos_t, sin_t = _rope_tables(positions)
    consts = _const_tables()
    params = _prep_params(norm_g, w_in, attn_q_norm, attn_k_norm, gla_w_gate_up, gla_b_gate, gla_out_norm,
                          ssd_conv_w, ssd_conv_b, ssd_dt_bias, ssd_A_log, ssd_D, ssd_out_norm,
                          w_branch, w_out)
    for layer in range(w_in.shape[0]):
        x = _layer_call(x, cos_t, sin_t, attn_sinks, params, consts, layer, ts)
    return x
```

```python
import functools

import numpy as np
import jax
import jax.numpy as jnp
from jax import lax
from jax.experimental import pallas as pl
from jax.experimental.pallas import tpu as pltpu

F32 = jnp.float32
BF16 = jnp.bfloat16

D_MODEL = 1024
EPS = 1e-6
ROPE_THETA = 10000.0

ATTN_HEADS = 8
ATTN_KV_HEADS = 2
ATTN_HEAD_DIM = 64
ATTN_BLOCK = 128
GLA_HEADS = 4
GLA_DK = 64
GLA_DV = 128
GLA_GATE_RANK = 16
GLA_GATE_NORMALIZER = 16.0
SSD_HEADS = 8
SSD_HEAD_DIM = 64
SSD_GROUPS = 2
SSD_D_STATE = 128
SSD_D_INNER = 512
SSD_CONV = 4

IN_PROJ_SIZES = (512, 128, 128, 512, 256, 256, 512, 512, 16, 1024, 8, 512, 3072)
IN_PROJ_CUTS = tuple(int(v) for v in np.cumsum((0,) + IN_PROJ_SIZES))

LANES = 128
CHUNK = 128
CONV_PAD = 8
SEQ_BLOCK = 512
PREP_ROWS = 256
VMEM_LIMIT_BYTES = 56 * 1024 * 1024
MASK_VALUE = -1e30
MXU_TILE = 256
SOFTMAX_ROWS = 64
ROPE_TOKENS = 4
CONV_STRIDE = 4
CONV_GROUP = 8 * CONV_STRIDE

W_Q, W_KV, W_GA, W_GQK, W_GV, W_GG, W_SM, W_XBC, W_Z, W_M, W_END = (
    0, 512, 768, 1280, 1792, 2304, 2816, 2944, 3968, 4480, 7552)
P_NORM, P_CONVB, P_CONVW = (0, 0, 1024), (1, 0, 1024), (2, 0, 1024)
P_QGAIN, P_KGAIN, P_DTB, P_ALOG, P_BGATE = (6, 0, 128), (6, 128, 128), (6, 256, 128), (6, 384, 128), (6, 512, 256)
P_GNORM, P_DEXP = (7, 0, 512), (7, 512, 512)
P_SNORM = (8, 0, 512)
P_ROWS = 16


def _mm(a, b):
    return jnp.dot(a, b, preferred_element_type=F32)


def _mm_nt(a, b):
    return lax.dot_general(a, b, (((1,), (1,)), ((), ())), preferred_element_type=F32)


def _mm_tn(a, b):
    return lax.dot_general(a, b, (((0,), (0,)), ((), ())), preferred_element_type=F32)


def _silu_of_double(vh):
    return vh * jnp.tanh(vh) + vh


def _softplus(v):
    return jnp.maximum(v, 0.0) + jnp.log(1.0 + jnp.exp(-jnp.abs(v)))


def _split3(v):
    hi = v.astype(BF16)
    r1 = v - hi.astype(F32)
    mid = r1.astype(BF16)
    lo = (r1 - mid.astype(F32)).astype(BF16)
    return hi, mid, lo


def _rope_table_kernel(pos_ref, freq_ref, cos_ref, sin_ref):
    ang = pos_ref[0].astype(F32) * freq_ref[...]
    cos_ref[0] = jnp.cos(ang)
    sin_ref[0] = jnp.sin(ang)


def _layer_kernel(
        sinks_ref, x_ref, cos_ref, sin_ref, w_ref, wbr_ref, wo_ref, wup_ref, prm_ref,
        gmat_ref, ltri_ref, emat_ref,
        out_ref,
        hb_scr, q_scr, k_scr, v_scr, bra_scr, brb_scr, brc_scr, merged_scr, conv_scr, xbc_scr,
        gla_scr, ssd_scr, gqk_scr, gv_scr, psm_scr, sga_scr, sgg_scr, sz_scr, gate_ac_scr, gate_b_scr,
        ya_scr, yb_scr, cos_scr, sin_scr,
        *, ts, layer):
    nblk = ts // ATTN_BLOCK
    nchunk = ts // CHUNK
    j = pl.program_id(1)

    @pl.when(j == 0)
    def _reset_carries():
        k_scr[0:ATTN_BLOCK, :] = jnp.zeros((ATTN_BLOCK, LANES), BF16)
        v_scr[0:ATTN_BLOCK, :] = jnp.zeros((ATTN_BLOCK, LANES), BF16)
        conv_scr[:, 0:CONV_PAD, :] = jnp.zeros((conv_scr.shape[0], CONV_PAD, LANES), F32)
        gla_scr[...] = jnp.zeros(gla_scr.shape, F32)
        ssd_scr[...] = jnp.zeros(ssd_scr.shape, F32)

    def prm(p):
        return prm_ref[p[0]:p[0] + 1, p[1]:p[1] + p[2]]

    def proj_tile(c0, dst_ref, d0, fn=None, width=MXU_TILE):
        def run():
            t = _mm(hb_scr[...], w_ref[:, c0:c0 + width])
            dst_ref[:, d0:d0 + width] = (t if fn is None else fn(t)).astype(dst_ref.dtype)
        return run

    def conv_in_tile(c0):
        def run():
            t = _mm(hb_scr[...], w_ref[:, W_XBC + c0:W_XBC + c0 + MXU_TILE])
            for i in range(MXU_TILE // LANES):
                conv_scr[c0 // LANES + i, CONV_PAD:CONV_PAD + ts, :] = t[:, i * LANES:(i + 1) * LANES]
        return run

    def merge_tile(y_ref, branch, gate_ref, c0, first):
        def run():
            u = _mm(y_ref[...], wbr_ref[branch, :, c0:c0 + MXU_TILE])
            g = gate_ref[:, c0:c0 + MXU_TILE] * u + u
            if first:
                merged_scr[:, c0:c0 + MXU_TILE] = g
            else:
                merged_scr[:, c0:c0 + MXU_TILE] += g
        return run

    def drain(tasks, count):
        for _ in range(min(count, len(tasks))):
            tasks.pop(0)()

    x = x_ref[0]
    ms = jnp.mean(x * x, axis=-1, keepdims=True)
    hb_scr[...] = (x * lax.rsqrt(ms + EPS) * prm(P_NORM)).astype(BF16)
    hb = hb_scr[...]

    lane = lax.broadcasted_iota(jnp.int32, (ATTN_BLOCK, LANES), 1)
    low_half = lane < LANES // 2

    lane_q = lax.broadcasted_iota(jnp.int32, (ts // ROPE_TOKENS, LANES), 1) // (LANES // ROPE_TOKENS)
    for src_ref, dst_scr in ((cos_ref, cos_scr), (sin_ref, sin_scr)):
        compact = src_ref[0]
        for p in range(ROPE_TOKENS):
            one = jnp.where(lane_q == p, compact, 0.0)
            two = one + pltpu.roll(one, LANES // 4, 1)
            dst_scr[pl.ds(p, ts // ROPE_TOKENS, stride=ROPE_TOKENS), :] = two + pltpu.roll(two, LANES // 2, 1)
    lane_ts = lax.broadcasted_iota(jnp.int32, (ts, LANES), 1)
    cos_t = cos_scr[...]
    sin_t = jnp.where(lane_ts < LANES // 2, -sin_scr[...], sin_scr[...])
    gmat = gmat_ref[...]
    akv = _mm(hb, w_ref[:, W_KV:W_GA])
    ak = akv[:, :LANES]
    kn = ak * lax.rsqrt(_mm((ak * ak).astype(BF16), gmat) + EPS) * prm(P_KGAIN)
    kr = kn * cos_t + pltpu.roll(kn, LANES // 2, 1) * sin_t
    k_scr[ATTN_BLOCK:ATTN_BLOCK + ts, :] = kr.astype(BF16)
    v_scr[ATTN_BLOCK:ATTN_BLOCK + ts, :] = akv[:, LANES:].astype(BF16)

    aq = _mm(hb, w_ref[:, W_Q:W_KV])
    group0 = (lane_ts // 32) % 2 == 0
    qgain = prm(P_QGAIN) * (ATTN_HEAD_DIM ** -0.5)
    for c in range(4):
        qc = aq[:, c * LANES:(c + 1) * LANES]
        qn = qc * lax.rsqrt(_mm((qc * qc).astype(BF16), gmat) + EPS) * qgain
        qr = qn * cos_t + pltpu.roll(qn, LANES // 2, 1) * sin_t
        q_scr[0, c * ts:(c + 1) * ts, :] = jnp.where(group0, qr, 0.0).astype(BF16)
        q_scr[1, c * ts:(c + 1) * ts, :] = jnp.where(group0, 0.0, qr).astype(BF16)

    tasks = [proj_tile(W_SM, psm_scr, 0, width=LANES)]
    tasks += [proj_tile(W_GQK + t * MXU_TILE, gqk_scr, t * MXU_TILE) for t in range(2)]
    tasks += [proj_tile(W_GV + t * MXU_TILE, gv_scr, t * MXU_TILE) for t in range(2)]
    tasks += [conv_in_tile(t * MXU_TILE) for t in range(4)]
    drain(tasks, len(tasks))
    tasks = [proj_tile(W_GA + t * MXU_TILE, sga_scr, t * MXU_TILE, _silu_of_double) for t in range(2)]
    tasks += [proj_tile(W_GG + t * MXU_TILE, sgg_scr, t * MXU_TILE, _silu_of_double) for t in range(2)]
    tasks += [proj_tile(W_Z + t * MXU_TILE, sz_scr, t * MXU_TILE, _silu_of_double) for t in range(2)]
    tasks += [proj_tile(W_M + t * MXU_TILE, gate_ac_scr, t * MXU_TILE, jnp.tanh) for t in range(4)]
    per_blk = -(-len(tasks) // nblk)

    qi = lax.broadcasted_iota(jnp.int32, (ATTN_BLOCK, ATTN_BLOCK), 0)
    ki = lax.broadcasted_iota(jnp.int32, (ATTN_BLOCK, ATTN_BLOCK), 1)
    from_prev = ki > qi
    first_block_bias = jnp.where(j > 0, 0.0, MASK_VALUE)
    for n in range(nblk):
        kb = k_scr[n * ATTN_BLOCK:(n + 2) * ATTN_BLOCK, :]
        vb = v_scr[n * ATTN_BLOCK:(n + 2) * ATTN_BLOCK, :]
        outs = []
        for g in range(ATTN_KV_HEADS):
            lhs = jnp.concatenate(
                [q_scr[g, c * ts + n * ATTN_BLOCK:c * ts + (n + 1) * ATTN_BLOCK, :] for c in range(4)],
                axis=0)
            s_all = _mm_nt(lhs, kb)
            probs, scales = [], []
            for r in range(4 * ATTN_BLOCK // SOFTMAX_ROWS):
                r0 = r * SOFTMAX_ROWS
                q0 = r0 % ATTN_BLOCK
                sink = sinks_ref[layer, 4 * g + r0 // ATTN_BLOCK]
                s_prev = s_all[r0:r0 + SOFTMAX_ROWS, :ATTN_BLOCK]
                if n == 0:
                    s_prev = s_prev + first_block_bias
                prev_sel = from_prev[q0:q0 + SOFTMAX_ROWS]
                s = jnp.where(prev_sel, s_prev, s_all[r0:r0 + SOFTMAX_ROWS, ATTN_BLOCK:])
                m = jnp.maximum(jnp.max(s, axis=1, keepdims=True), sink)
                p = jnp.exp(s - m)
                denom = jnp.sum(p, axis=1, keepdims=True) + jnp.exp(sink - m)
                pb = p.astype(BF16)
                zero = jnp.zeros_like(pb)
                probs.append(jnp.concatenate(
                    [jnp.where(prev_sel, pb, zero), jnp.where(prev_sel, zero, pb)], axis=1))
                scales.append(1.0 / denom)
            outs.append(_mm(jnp.concatenate(probs, axis=0), vb) * jnp.concatenate(scales, axis=0))
            drain(tasks, (per_blk + 1 - g) // 2)
        for c in range(4):
            bra_scr[n * ATTN_BLOCK:(n + 1) * ATTN_BLOCK, c * LANES:(c + 1) * LANES] = jnp.where(
                low_half,
                outs[0][c * ATTN_BLOCK:(c + 1) * ATTN_BLOCK],
                outs[1][c * ATTN_BLOCK:(c + 1) * ATTN_BLOCK])
    drain(tasks, len(tasks))
    k_scr[0:ATTN_BLOCK, :] = k_scr[ts:ts + ATTN_BLOCK, :]
    v_scr[0:ATTN_BLOCK, :] = v_scr[ts:ts + ATTN_BLOCK, :]
    ya_scr[...] = (bra_scr[...] * sga_scr[...]).astype(BF16)

    psm = psm_scr[...]
    logit = _mm(psm.astype(BF16), wup_ref[...]) + prm(P_BGATE)
    log_alpha = (jnp.minimum(logit, 0.0) - jnp.log(1.0 + jnp.exp(-jnp.abs(logit)))) * (
        1.0 / GLA_GATE_NORMALIZER)
    dt = _softplus(psm + prm(P_DTB))
    adt = dt * (-jnp.exp(prm(P_ALOG)))
    cs_in = jnp.concatenate([log_alpha, adt], axis=1)
    ltri = ltri_ref[...]
    cs_rows = ltri.shape[0]
    cs_parts = []
    for r in range(ts // cs_rows):
        pieces = _split3(cs_in[r * cs_rows:(r + 1) * cs_rows])
        cs_parts.append(_mm(ltri, pieces[0]) + _mm(ltri, pieces[1]) + _mm(ltri, pieces[2]))
    cs = cs_parts[0] if len(cs_parts) == 1 else jnp.concatenate(cs_parts, axis=0)
    bcum = cs[:, :GLA_HEADS * GLA_DK]
    acs = cs[:, GLA_HEADS * GLA_DK:]

    tasks = [merge_tile(ya_scr, 0, gate_ac_scr, t * MXU_TILE, True) for t in range(4)]
    tasks += [proj_tile(W_M + D_MODEL + t * MXU_TILE, gate_b_scr, t * MXU_TILE, jnp.tanh)
              for t in range(4)]
    per_chunk = -(-len(tasks) // nchunk)

    dkw = GLA_HEADS * GLA_DK
    lane_k = lax.broadcasted_iota(jnp.int32, (CHUNK, dkw), 1) // GLA_DK
    ci = lax.broadcasted_iota(jnp.int32, (GLA_HEADS * CHUNK, CHUNK), 0) % CHUNK
    cj = lax.broadcasted_iota(jnp.int32, (GLA_HEADS * CHUNK, CHUNK), 1)
    causal4 = ci >= cj
    st_blockdiag = (lax.broadcasted_iota(jnp.int32, gla_scr.shape, 0) // GLA_DV
                    == lax.broadcasted_iota(jnp.int32, gla_scr.shape, 1) // GLA_DK)
    gla_state = gla_scr[...]
    for c in range(nchunk):
        rs = slice(c * CHUNK, (c + 1) * CHUNK)
        bc = bcum[rs]
        bmid = bc[CHUNK // 2 - 1:CHUNK // 2]
        blast = bc[CHUNK - 1:CHUNK]
        q = gqk_scr[rs, :dkw] * (GLA_DK ** -0.5)
        k = gqk_scr[rs, dkw:]
        v = gv_scr[rs, :]
        q_dec = (q * jnp.exp(bc)).astype(BF16)
        q_mid = q * jnp.exp(bc - bmid)
        k_mid = (k * jnp.exp(bmid - bc)).astype(BF16)
        k_end = (k * jnp.exp(blast - bc)).astype(BF16)
        lhs = jnp.concatenate(
            [jnp.where(lane_k == h, q_mid, 0.0).astype(BF16) for h in range(GLA_HEADS)], axis=0)
        attn = jnp.where(causal4, _mm_nt(lhs, k_mid), 0.0).astype(BF16)
        o_inter = _mm_nt(q_dec, gla_state.astype(BF16))
        drain(tasks, per_chunk // 2)
        for h in range(GLA_HEADS):
            hs = slice(h * GLA_DV, (h + 1) * GLA_DV)
            o = _mm(attn[h * CHUNK:(h + 1) * CHUNK], v[:, hs]) + o_inter[:, hs]
            brb_scr[rs, hs] = o * lax.rsqrt(jnp.mean(o * o, axis=-1, keepdims=True) + EPS)
        gla_state = jnp.where(st_blockdiag, gla_state * jnp.exp(blast) + _mm_tn(v, k_end), 0.0)
        drain(tasks, per_chunk - per_chunk // 2)
    drain(tasks, len(tasks))
    gla_scr[...] = gla_state
    yb_scr[...] = (brb_scr[...] * prm(P_GNORM) * sgg_scr[...]).astype(BF16)

    tasks = [merge_tile(yb_scr, 1, gate_b_scr, t * MXU_TILE, False) for t in range(4)]
    tasks += [proj_tile(W_M + 2 * D_MODEL + t * MXU_TILE, gate_ac_scr, t * MXU_TILE, jnp.tanh)
              for t in range(4)]
    per_chunk = -(-len(tasks) // nchunk)

    emat = emat_ref[...]
    tril = (lax.broadcasted_iota(jnp.int32, (CHUNK, CHUNK), 0)
            >= lax.broadcasted_iota(jnp.int32, (CHUNK, CHUNK), 1))
    gw = SSD_D_INNER // SSD_GROUPS
    ssd_state = ssd_scr[...]
    nslab = conv_scr.shape[0]
    row = lambda r, sl: jnp.broadcast_to(prm_ref[r:r + 1, sl * LANES:(sl + 1) * LANES] * 0.5, (8, LANES))
    half_convb = [row(P_CONVB[0], sl) for sl in range(nslab)]
    half_convw = [[row(P_CONVW[0] + i, sl) for i in range(SSD_CONV)] for sl in range(nslab)]
    for c in range(nchunk):
        rs = slice(c * CHUNK, (c + 1) * CHUNK)
        for sl in range(nslab):
            for grp in range(CHUNK // CONV_GROUP):
                base = CONV_PAD + c * CHUNK + grp * CONV_GROUP
                taps = [conv_scr[sl, pl.ds(base + m, 8, stride=CONV_STRIDE), :]
                        for m in range(1 - SSD_CONV, CONV_STRIDE)]
                for m in range(CONV_STRIDE):
                    acc = half_convb[sl]
                    for i in range(SSD_CONV):
                        acc = acc + half_convw[sl][i] * taps[m + i]
                    xbc_scr[sl, pl.ds(c * CHUNK + grp * CONV_GROUP + m, 8, stride=CONV_STRIDE), :] = (
                        _silu_of_double(acc))
        nx = SSD_D_INNER // LANES
        xs = jnp.concatenate([xbc_scr[sl, rs, :] for sl in range(nx)], axis=1)
        bmat = [xbc_scr[nx + g, rs, :].astype(BF16) for g in range(SSD_GROUPS)]
        cmat = [xbc_scr[nx + SSD_GROUPS + g, rs, :] for g in range(SSD_GROUPS)]
        a = acs[rs]
        a_last = a[CHUNK - 1:CHUNK]
        w = dt[rs] * jnp.exp(a_last - a)
        w_hi = w.astype(BF16)
        w_lo = (w - w_hi.astype(F32)).astype(BF16)
        xw = (xs * (_mm(w_hi, emat) + _mm(w_lo, emat))).astype(BF16)
        al3 = _split3(jnp.broadcast_to(a_last, (8, LANES)))
        chunk_decay = jnp.exp(_mm(al3[0], emat) + _mm(al3[1], emat) + _mm(al3[2], emat))[0:1]
        a_t = a.T
        dt_t = dt[rs].T
        x_c = xs.astype(BF16)
        state_b = ssd_state.astype(BF16)
        drain(tasks, per_chunk // 2)
        ys, new_state = [], []
        for g in range(SSD_GROUPS):
            b_g = bmat[g]
            c_g = cmat[g]
            cb = _mm_nt(c_g.astype(BF16), b_g)
            for hh in range(SSD_HEADS // SSD_GROUPS):
                h = g * (SSD_HEADS // SSD_GROUPS) + hh
                a_col = a[:, h:h + 1]
                decay = jnp.where(tril, jnp.exp(a_col - a_t[h:h + 1, :]), 0.0)
                m_h = cb * decay * dt_t[h:h + 1, :]
                lhs = jnp.concatenate([m_h, c_g * jnp.exp(a_col)], axis=1).astype(BF16)
                ps = slice((h // 2) * LANES, (h // 2 + 1) * LANES)
                rhs = jnp.concatenate([x_c[:, ps], state_b[:, ps]], axis=0)
                ys.append(_mm(lhs, rhs))
            ws = slice(g * gw, (g + 1) * gw)
            new_state.append(ssd_state[:, ws] * chunk_decay[:, ws] + _mm_tn(b_g, xw[:, ws]))
        ssd_state = jnp.concatenate(new_state, axis=1)
        y = jnp.concatenate(
            [jnp.where(low_half, ys[2 * p], ys[2 * p + 1]) for p in range(SSD_HEADS // 2)], axis=1)
        u = (y + xs * prm(P_DEXP)) * sz_scr[rs, :]
        normed = []
        for g in range(SSD_GROUPS):
            ug = u[:, g * gw:(g + 1) * gw]
            normed.append(ug * lax.rsqrt(jnp.mean(ug * ug, axis=-1, keepdims=True) + EPS))
        brc_scr[rs, :] = (jnp.concatenate(normed, axis=1) * prm(P_SNORM)).astype(BF16)
        drain(tasks, per_chunk - per_chunk // 2)
    drain(tasks, len(tasks))
    ssd_scr[...] = ssd_state
    conv_scr[:, 0:CONV_PAD, :] = conv_scr[:, ts:ts + CONV_PAD, :]

    u_c = _mm(brc_scr[...], wbr_ref[2])
    merged = merged_scr[...] + (gate_ac_scr[...] * u_c + u_c)
    out_ref[0] = x_ref[0] + _mm(merged.astype(BF16), wo_ref[...])


def _to_ya_lanes(w, axis):
    shape = w.shape
    w = w.reshape(*shape[:axis], ATTN_KV_HEADS, 4, ATTN_HEAD_DIM, *shape[axis + 1:])
    return jnp.swapaxes(w, axis, axis + 1).reshape(shape)


def _const_tables():
    lane = np.arange(LANES)
    gmat = ((lane[:, None] // 32) % 2 == (lane[None, :] // 32) % 2).astype(np.float32) / ATTN_HEAD_DIM
    r = np.arange(2 * CHUNK)
    ltri = ((r[:, None] // CHUNK == r[None, :] // CHUNK) & (r[:, None] >= r[None, :])).astype(np.float32)
    emat = np.zeros((LANES, SSD_D_INNER), np.float32)
    for h in range(SSD_HEADS):
        emat[h, h * SSD_HEAD_DIM:(h + 1) * SSD_HEAD_DIM] = 1.0
    return (jnp.asarray(gmat, BF16), jnp.asarray(ltri, BF16), jnp.asarray(emat, BF16))


def _weight_prep_kernel(w_ref, o_ref):
    c = IN_PROJ_CUTS

    def put(dst, srcs, width, scale=None):
        pieces = [w_ref[:, s:s + width] for s in srcs]
        v = pieces[0] if len(pieces) == 1 else jnp.concatenate(pieces, axis=1)
        o_ref[:, dst:dst + v.shape[1]] = (v if scale is None else v * scale).astype(BF16)

    for blk in range(4):
        put(W_Q + blk * LANES, [c[0] + (blk + 4 * (r % 2)) * 64 + 32 * (r // 2) for r in range(4)], 32)
        put(W_GA + blk * LANES, [c[3] + (blk + 4 * g) * 64 for g in range(2)], 64, 0.5)
    put(W_KV, [c[1] + (r % 2) * 64 + 32 * (r // 2) for r in range(4)], 32)
    put(W_KV + LANES, [c[2]], LANES)
    put(W_GQK, [c[4]], 512)
    put(W_GV, [c[6]], 512)
    put(W_GG, [c[7]], 512, 0.5)
    small = jnp.concatenate([w_ref[:, c[10]:c[11]], w_ref[:, c[8]:c[9]],
                             jnp.zeros((w_ref.shape[0], LANES - 24), F32)], axis=1)
    o_ref[:, W_SM:W_SM + LANES] = small.astype(BF16)
    put(W_XBC, [c[9]], 1024)
    put(W_Z, [c[11]], 512, 0.5)
    put(W_M, [c[12]], 3 * D_MODEL, 0.5)


def _prep_in_proj(w_in):
    depth, d_in, n = w_in.shape
    return pl.pallas_call(
        _weight_prep_kernel,
        grid=(depth, d_in // PREP_ROWS),
        in_specs=[pl.BlockSpec((None, PREP_ROWS, n), lambda l, r: (l, r, 0))],
        out_specs=pl.BlockSpec((None, PREP_ROWS, W_END), lambda l, r: (l, r, 0)),
        out_shape=jax.ShapeDtypeStruct((depth, d_in, W_END), BF16),
        compiler_params=pltpu.CompilerParams(vmem_limit_bytes=VMEM_LIMIT_BYTES),
        name="weight_prep",
    )(w_in)


def _prep_params(norm_g, w_in, attn_q_norm, attn_k_norm, gla_w_gate_up, gla_b_gate, gla_out_norm,
                 ssd_conv_w, ssd_conv_b, ssd_dt_bias, ssd_A_log, ssd_D, ssd_out_norm, w_branch, w_out):
    depth = w_in.shape[0]
    w_cat = _prep_in_proj(w_in)
    wbr = (0.5 * jnp.concatenate([_to_ya_lanes(w_branch[:, 0:1], 2), w_branch[:, 1:]], axis=1)).astype(BF16)
    wup = jnp.zeros((depth, LANES, gla_w_gate_up.shape[-1]), F32).at[:, 8:8 + GLA_GATE_RANK].set(
        gla_w_gate_up).astype(BF16)
    lane_gain = lambda g: jnp.repeat(g.reshape(depth, 2, 1, 32), 2, axis=2).reshape(depth, LANES)
    pad = lambda v, n: jnp.pad(v.astype(F32), ((0, 0), (0, n - v.shape[-1])))
    prm = jnp.zeros((depth, P_ROWS, D_MODEL), F32)
    prm = prm.at[:, P_NORM[0]].set(norm_g)
    prm = prm.at[:, P_CONVB[0]].set(ssd_conv_b)
    prm = prm.at[:, P_CONVW[0]:P_CONVW[0] + SSD_CONV].set(ssd_conv_w)
    prm = prm.at[:, P_QGAIN[0]].set(jnp.concatenate([
        lane_gain(attn_q_norm), lane_gain(attn_k_norm), pad(ssd_dt_bias, LANES), pad(ssd_A_log, LANES),
        gla_b_gate, jnp.zeros((depth, D_MODEL - 768), F32)], axis=-1))
    prm = prm.at[:, P_GNORM[0]].set(jnp.concatenate([
        jnp.tile(gla_out_norm, (1, GLA_HEADS)), jnp.repeat(ssd_D, SSD_HEAD_DIM, axis=-1)], axis=-1))
    prm = prm.at[:, P_SNORM[0], :SSD_D_INNER].set(ssd_out_norm)
    return w_cat, wbr, w_out.astype(BF16), wup, prm


def _layer_call(x, cos_t, sin_t, sinks, params, consts, layer, ts):
    bsz, seq, d = x.shape
    tok = lambda w: pl.BlockSpec((1, ts, w), lambda b, j: (b, j, 0))

    def layer_spec(arr):
        nd = arr.ndim - 1
        return pl.BlockSpec((None,) + arr.shape[1:], lambda b, j, _nd=nd: (layer,) + (0,) * _nd,
                            pipeline_mode=pl.Buffered(1))

    def const_spec(arr):
        return pl.BlockSpec(arr.shape, lambda b, j, _nd=arr.ndim: (0,) * _nd, pipeline_mode=pl.Buffered(1))

    rope_spec = pl.BlockSpec((1, ts // ROPE_TOKENS, LANES), lambda b, j: (b, j, 0))
    in_specs = [pl.BlockSpec(memory_space=pltpu.SMEM), tok(d), rope_spec, rope_spec]
    in_specs += [layer_spec(a) for a in params] + [const_spec(a) for a in consts]
    conv_slabs = (SSD_D_INNER + 2 * SSD_GROUPS * SSD_D_STATE) // LANES
    scratch = [
        pltpu.VMEM((ts, d), BF16),
        pltpu.VMEM((ATTN_KV_HEADS, 4 * ts, LANES), BF16),
        pltpu.VMEM((ts + ATTN_BLOCK, LANES), BF16),
        pltpu.VMEM((ts + ATTN_BLOCK, LANES), BF16),
        pltpu.VMEM((ts, 512), F32),
        pltpu.VMEM((ts, 512), F32),
        pltpu.VMEM((ts, 512), BF16),
        pltpu.VMEM((ts, d), F32),
        pltpu.VMEM((conv_slabs, ts + CONV_PAD, LANES), F32),
        pltpu.VMEM((conv_slabs, ts, LANES), F32),
        pltpu.VMEM((GLA_HEADS * GLA_DV, GLA_HEADS * GLA_DK), F32),
        pltpu.VMEM((SSD_D_STATE, SSD_D_INNER), F32),
        pltpu.VMEM((ts, 512), F32),
        pltpu.VMEM((ts, 512), BF16),
        pltpu.VMEM((ts, LANES), F32),
        pltpu.VMEM((ts, 512), F32),
        pltpu.VMEM((ts, 512), F32),
        pltpu.VMEM((ts, 512), F32),
        pltpu.VMEM((ts, d), F32),
        pltpu.VMEM((ts, d), F32),
        pltpu.VMEM((ts, 512), BF16),
        pltpu.VMEM((ts, 512), BF16),
        pltpu.VMEM((ts, LANES), F32),
        pltpu.VMEM((ts, LANES), F32),
    ]
    return pl.pallas_call(
        functools.partial(_layer_kernel, ts=ts, layer=layer),
        grid=(bsz, seq // ts),
        in_specs=in_specs,
        out_specs=pl.BlockSpec((1, ts, d), lambda b, j: (b, j, 0)),
        out_shape=jax.ShapeDtypeStruct(x.shape, x.dtype),
        scratch_shapes=scratch,
        compiler_params=pltpu.CompilerParams(
            dimension_semantics=("arbitrary", "arbitrary"),
            vmem_limit_bytes=VMEM_LIMIT_BYTES),
        name="hybrid_layer",
    )(sinks.astype(F32), x, cos_t, sin_t, *params, *consts)


def _rope_tables(positions):
    bsz, seq = positions.shape
    nfreq = ATTN_HEAD_DIM // 2
    per_row = LANES // nfreq
    inv_freq = ROPE_THETA ** (-jnp.arange(0, ATTN_HEAD_DIM, 2, dtype=F32) / ATTN_HEAD_DIM)
    freq = jnp.tile(inv_freq, per_row).reshape(1, LANES)
    pos = jnp.repeat(positions.reshape(bsz, seq // per_row, per_row), nfreq, axis=2)
    spec = pl.BlockSpec((1, seq // per_row, LANES), lambda b: (b, 0, 0))
    return pl.pallas_call(
        _rope_table_kernel,
        grid=(bsz,),
        in_specs=[spec, pl.BlockSpec((1, LANES), lambda b: (0, 0))],
        out_specs=[spec, spec],
        out_shape=[jax.ShapeDtypeStruct((bsz, seq // per_row, LANES), F32)] * 2,
        name="rope_tables",
    )(pos, freq)


def kernel(x, positions, norm_g, w_in, attn_q_norm, attn_k_norm, attn_sinks, gla_w_gate_up, gla_b_gate,
           gla_out_norm, ssd_conv_w, ssd_conv_b, ssd_dt_bias, ssd_A_log, ssd_D, ssd_out_norm,
           w_branch, w_out):
    seq = x.shape[1]
    ts = min(SEQ_BLOCK, seq)
    assert seq % ts == 0 and ts % (2 * CHUNK) == 0 and x.shape[2] == D_MODEL
    cos_t, sin_t = _rope_tables(positions)
    consts = _const_tables()
    params = _prep_params(norm_g, w_in, attn_q_norm, attn_k_norm, gla_w_gate_up, gla_b_gate, gla_out_norm,
                          ssd_conv_w, ssd_conv_b, ssd_dt_bias, ssd_A_log, ssd_D, ssd_out_norm,
                          w_branch, w_out)
    for layer in range(w_in.shape[0]):
        x = _layer_call(x, cos_t, sin_t, attn_sinks, params, consts, layer, ts)
    return x
```

```python
import functools

import numpy as np
import jax
import jax.numpy as jnp
from jax import lax
from jax.experimental import pallas as pl
from jax.experimental.pallas import tpu as pltpu

F32 = jnp.float32
BF16 = jnp.bfloat16

D_MODEL = 1024
EPS = 1e-6
ROPE_THETA = 10000.0

ATTN_HEADS = 8
ATTN_KV_HEADS = 2
ATTN_HEAD_DIM = 64
ATTN_BLOCK = 128
GLA_HEADS = 4
GLA_DK = 64
GLA_DV = 128
GLA_GATE_RANK = 16
GLA_GATE_NORMALIZER = 16.0
SSD_HEADS = 8
SSD_HEAD_DIM = 64
SSD_GROUPS = 2
SSD_D_STATE = 128
SSD_D_INNER = 512
SSD_CONV = 4

IN_PROJ_SIZES = (512, 128, 128, 512, 256, 256, 512, 512, 16, 1024, 8, 512, 3072)
IN_PROJ_CUTS = tuple(int(v) for v in np.cumsum((0,) + IN_PROJ_SIZES))

LANES = 128
CHUNK = 128
CONV_PAD = 8
SEQ_BLOCK = 512
PREP_ROWS = 256
VMEM_LIMIT_BYTES = 56 * 1024 * 1024
MASK_VALUE = -1e30
MXU_TILE = 256
SOFTMAX_ROWS = 64
ROPE_TOKENS = 4
CONV_STRIDE = 4
CONV_GROUP = 8 * CONV_STRIDE

W_Q, W_KV, W_GA, W_GQK, W_GV, W_GG, W_SM, W_XBC, W_Z, W_M, W_END = (
    0, 512, 768, 1280, 1792, 2304, 2816, 2944, 3968, 4480, 7552)
P_NORM, P_CONVB, P_CONVW = (0, 0, 1024), (1, 0, 1024), (2, 0, 1024)
P_QGAIN, P_KGAIN, P_DTB, P_ALOG, P_BGATE = (6, 0, 128), (6, 128, 128), (6, 256, 128), (6, 384, 128), (6, 512, 256)
P_GNORM, P_DEXP = (7, 0, 512), (7, 512, 512)
P_SNORM = (8, 0, 512)
P_ROWS = 16


def _mm(a, b):
    return jnp.dot(a, b, preferred_element_type=F32)


def _mm_nt(a, b):
    return lax.dot_general(a, b, (((1,), (1,)), ((), ())), preferred_element_type=F32)


def _mm_tn(a, b):
    return lax.dot_general(a, b, (((0,), (0,)), ((), ())), preferred_element_type=F32)


def _silu_of_double(vh):
    return vh * jnp.tanh(vh) + vh


def _softplus(v):
    return jnp.maximum(v, 0.0) + jnp.log(1.0 + jnp.exp(-jnp.abs(v)))


def _split3(v):
    hi = v.astype(BF16)
    r1 = v - hi.astype(F32)
    mid = r1.astype(BF16)
    lo = (r1 - mid.astype(F32)).astype(BF16)
    return hi, mid, lo


def _rope_table_kernel(pos_ref, freq_ref, cos_ref, sin_ref):
    ang = pos_ref[0].astype(F32) * freq_ref[...]
    cos_ref[0] = jnp.cos(ang)
    sin_ref[0] = jnp.sin(ang)


def _layer_kernel(
        sinks_ref, x_ref, cos_ref, sin_ref, w_ref, wbr_ref, wo_ref, wup_ref, prm_ref,
        gmat_ref, ltri_ref, emat_ref,
        out_ref,
        hb_scr, q_scr, k_scr, v_scr, bra_scr, brb_scr, brc_scr, merged_scr, conv_scr, xbc_scr,
        gla_scr, ssd_scr, gqk_scr, gv_scr, psm_scr, sga_scr, sgg_scr, sz_scr, gate_ac_scr, gate_b_scr,
        ya_scr, yb_scr, cos_scr, sin_scr,
        *, ts, layer):
    nblk = ts // ATTN_BLOCK
    nchunk = ts // CHUNK
    j = pl.program_id(1)

    @pl.when(j == 0)
    def _reset_carries():
        k_scr[0:ATTN_BLOCK, :] = jnp.zeros((ATTN_BLOCK, LANES), BF16)
        v_scr[0:ATTN_BLOCK, :] = jnp.zeros((ATTN_BLOCK, LANES), BF16)
        conv_scr[:, 0:CONV_PAD, :] = jnp.zeros((conv_scr.shape[0], CONV_PAD, LANES), F32)
        gla_scr[...] = jnp.zeros(gla_scr.shape, F32)
        ssd_scr[...] = jnp.zeros(ssd_scr.shape, F32)

    def prm(p):
        return prm_ref[p[0]:p[0] + 1, p[1]:p[1] + p[2]]

    def proj_tile(c0, dst_ref, d0, fn=None, width=MXU_TILE):
        def run():
            t = _mm(hb_scr[...], w_ref[:, c0:c0 + width])
            dst_ref[:, d0:d0 + width] = (t if fn is None else fn(t)).astype(dst_ref.dtype)
        return run

    def conv_in_tile(c0):
        def run():
            t = _mm(hb_scr[...], w_ref[:, W_XBC + c0:W_XBC + c0 + MXU_TILE])
            for i in range(MXU_TILE // LANES):
                conv_scr[c0 // LANES + i, CONV_PAD:CONV_PAD + ts, :] = t[:, i * LANES:(i + 1) * LANES]
        return run

    def merge_tile(y_ref, branch, gate_ref, c0, first):
        def run():
            u = _mm(y_ref[...], wbr_ref[branch, :, c0:c0 + MXU_TILE])
            g = gate_ref[:, c0:c0 + MXU_TILE] * u + u
            if first:
                merged_scr[:, c0:c0 + MXU_TILE] = g
            else:
                merged_scr[:, c0:c0 + MXU_TILE] += g
        return run

    def drain(tasks, count):
        for _ in range(min(count, len(tasks))):
            tasks.pop(0)()

    half = ts // 2
    for r in range(2):
        x = x_ref[0, r * half:(r + 1) * half, :]
        ms = jnp.mean(x * x, axis=-1, keepdims=True)
        hb_scr[r * half:(r + 1) * half, :] = (x * lax.rsqrt(ms + EPS) * prm(P_NORM)).astype(BF16)

    lane = lax.broadcasted_iota(jnp.int32, (ATTN_BLOCK, LANES), 1)
    low_half = lane < LANES // 2

    lane_q = lax.broadcasted_iota(jnp.int32, (ts // ROPE_TOKENS, LANES), 1) // (LANES // ROPE_TOKENS)
    for src_ref, dst_scr in ((cos_ref, cos_scr), (sin_ref, sin_scr)):
        compact = src_ref[0]
        for p in range(ROPE_TOKENS):
            one = jnp.where(lane_q == p, compact, 0.0)
            two = one + pltpu.roll(one, LANES // 4, 1)
            dst_scr[pl.ds(p, ts // ROPE_TOKENS, stride=ROPE_TOKENS), :] = two + pltpu.roll(two, LANES // 2, 1)
    lane_ts = lax.broadcasted_iota(jnp.int32, (ts, LANES), 1)
    cos_t = cos_scr[...]
    sin_t = jnp.where(lane_ts < LANES // 2, -sin_scr[...], sin_scr[...])
    gmat = gmat_ref[...]
    akv = jnp.concatenate([_mm(hb_scr[r * half:(r + 1) * half, :], w_ref[:, W_KV:W_GA]) for r in range(2)],
                          axis=0)
    ak = akv[:, :LANES]
    kn = ak * lax.rsqrt(_mm((ak * ak).astype(BF16), gmat) + EPS) * prm(P_KGAIN)
    kr = kn * cos_t + pltpu.roll(kn, LANES // 2, 1) * sin_t
    k_scr[ATTN_BLOCK:ATTN_BLOCK + ts, :] = kr.astype(BF16)
    v_scr[ATTN_BLOCK:ATTN_BLOCK + ts, :] = akv[:, LANES:].astype(BF16)

    aq = jnp.concatenate([_mm(hb_scr[r * half:(r + 1) * half, :], w_ref[:, W_Q:W_KV]) for r in range(2)],
                         axis=0)
    group0 = (lane_ts // 32) % 2 == 0
    qgain = prm(P_QGAIN) * (ATTN_HEAD_DIM ** -0.5)
    for c in range(4):
        qc = aq[:, c * LANES:(c + 1) * LANES]
        qn = qc * lax.rsqrt(_mm((qc * qc).astype(BF16), gmat) + EPS) * qgain
        qr = qn * cos_t + pltpu.roll(qn, LANES // 2, 1) * sin_t
        q_scr[0, c * ts:(c + 1) * ts, :] = jnp.where(group0, qr, 0.0).astype(BF16)
        q_scr[1, c * ts:(c + 1) * ts, :] = jnp.where(group0, 0.0, qr).astype(BF16)

    tasks = [proj_tile(W_SM, psm_scr, 0, width=LANES)]
    tasks += [proj_tile(W_GQK + t * MXU_TILE, gqk_scr, t * MXU_TILE) for t in range(2)]
    tasks += [proj_tile(W_GV + t * MXU_TILE, gv_scr, t * MXU_TILE) for t in range(2)]
    tasks += [conv_in_tile(t * MXU_TILE) for t in range(4)]
    drain(tasks, len(tasks))
    tasks = [proj_tile(W_GA + t * MXU_TILE, sga_scr, t * MXU_TILE, _silu_of_double) for t in range(2)]
    tasks += [proj_tile(W_GG + t * MXU_TILE, sgg_scr, t * MXU_TILE, _silu_of_double) for t in range(2)]
    tasks += [proj_tile(W_Z + t * MXU_TILE, sz_scr, t * MXU_TILE, _silu_of_double) for t in range(2)]
    tasks += [proj_tile(W_M + t * MXU_TILE, gate_ac_scr, t * MXU_TILE, jnp.tanh) for t in range(4)]
    per_blk = -(-len(tasks) // nblk)

    qi = lax.broadcasted_iota(jnp.int32, (ATTN_BLOCK, ATTN_BLOCK), 0)
    ki = lax.broadcasted_iota(jnp.int32, (ATTN_BLOCK, ATTN_BLOCK), 1)
    from_prev = ki > qi
    first_block_bias = jnp.where(j > 0, 0.0, MASK_VALUE)
    for n in range(nblk):
        kb = k_scr[n * ATTN_BLOCK:(n + 2) * ATTN_BLOCK, :]
        vb = v_scr[n * ATTN_BLOCK:(n + 2) * ATTN_BLOCK, :]
        outs = []
        for g in range(ATTN_KV_HEADS):
            lhs = jnp.concatenate(
                [q_scr[g, c * ts + n * ATTN_BLOCK:c * ts + (n + 1) * ATTN_BLOCK, :] for c in range(4)],
                axis=0)
            s_all = _mm_nt(lhs, kb)
            probs, scales = [], []
            for r in range(4 * ATTN_BLOCK // SOFTMAX_ROWS):
                r0 = r * SOFTMAX_ROWS
                q0 = r0 % ATTN_BLOCK
                sink = sinks_ref[layer, 4 * g + r0 // ATTN_BLOCK]
                s_prev = s_all[r0:r0 + SOFTMAX_ROWS, :ATTN_BLOCK]
                if n == 0:
                    s_prev = s_prev + first_block_bias
                prev_sel = from_prev[q0:q0 + SOFTMAX_ROWS]
                s = jnp.where(prev_sel, s_prev, s_all[r0:r0 + SOFTMAX_ROWS, ATTN_BLOCK:])
                m = jnp.maximum(jnp.max(s, axis=1, keepdims=True), sink)
                p = jnp.exp(s - m)
                denom = jnp.sum(p, axis=1, keepdims=True) + jnp.exp(sink - m)
                pb = p.astype(BF16)
                zero = jnp.zeros_like(pb)
                probs.append(jnp.concatenate(
                    [jnp.where(prev_sel, pb, zero), jnp.where(prev_sel, zero, pb)], axis=1))
                scales.append(1.0 / denom)
            outs.append(_mm(jnp.concatenate(probs, axis=0), vb) * jnp.concatenate(scales, axis=0))
            drain(tasks, (per_blk + 1 - g) // 2)
        for c in range(4):
            bra_scr[n * ATTN_BLOCK:(n + 1) * ATTN_BLOCK, c * LANES:(c + 1) * LANES] = jnp.where(
                low_half,
                outs[0][c * ATTN_BLOCK:(c + 1) * ATTN_BLOCK],
                outs[1][c * ATTN_BLOCK:(c + 1) * ATTN_BLOCK])
    drain(tasks, len(tasks))
    k_scr[0:ATTN_BLOCK, :] = k_scr[ts:ts + ATTN_BLOCK, :]
    v_scr[0:ATTN_BLOCK, :] = v_scr[ts:ts + ATTN_BLOCK, :]
    ya_scr[...] = (bra_scr[...] * sga_scr[...]).astype(BF16)

    psm = psm_scr[...]
    logit = _mm(psm.astype(BF16), wup_ref[...]) + prm(P_BGATE)
    log_alpha = (jnp.minimum(logit, 0.0) - jnp.log(1.0 + jnp.exp(-jnp.abs(logit)))) * (
        1.0 / GLA_GATE_NORMALIZER)
    dt = _softplus(psm + prm(P_DTB))
    adt = dt * (-jnp.exp(prm(P_ALOG)))
    cs_in = jnp.concatenate([log_alpha, adt], axis=1)
    ltri = ltri_ref[...]
    cs_rows = ltri.shape[0]
    cs_parts = []
    for r in range(ts // cs_rows):
        pieces = _split3(cs_in[r * cs_rows:(r + 1) * cs_rows])
        cs_parts.append(_mm(ltri, pieces[0]) + _mm(ltri, pieces[1]) + _mm(ltri, pieces[2]))
    cs = cs_parts[0] if len(cs_parts) == 1 else jnp.concatenate(cs_parts, axis=0)
    bcum = cs[:, :GLA_HEADS * GLA_DK]
    acs = cs[:, GLA_HEADS * GLA_DK:]

    tasks = [merge_tile(ya_scr, 0, gate_ac_scr, t * MXU_TILE, True) for t in range(4)]
    tasks += [proj_tile(W_M + D_MODEL + t * MXU_TILE, gate_b_scr, t * MXU_TILE, jnp.tanh)
              for t in range(4)]
    per_chunk = -(-len(tasks) // nchunk)

    dkw = GLA_HEADS * GLA_DK
    lane_k = lax.broadcasted_iota(jnp.int32, (CHUNK, dkw), 1) // GLA_DK
    ci = lax.broadcasted_iota(jnp.int32, (GLA_HEADS * CHUNK, CHUNK), 0) % CHUNK
    cj = lax.broadcasted_iota(jnp.int32, (GLA_HEADS * CHUNK, CHUNK), 1)
    causal4 = ci >= cj
    st_blockdiag = (lax.broadcasted_iota(jnp.int32, gla_scr.shape, 0) // GLA_DV
                    == lax.broadcasted_iota(jnp.int32, gla_scr.shape, 1) // GLA_DK)
    gla_state = gla_scr[...]
    for c in range(nchunk):
        rs = slice(c * CHUNK, (c + 1) * CHUNK)
        bc = bcum[rs]
        bmid = bc[CHUNK // 2 - 1:CHUNK // 2]
        blast = bc[CHUNK - 1:CHUNK]
        q = gqk_scr[rs, :dkw] * (GLA_DK ** -0.5)
        k = gqk_scr[rs, dkw:]
        v = gv_scr[rs, :]
        q_dec = (q * jnp.exp(bc)).astype(BF16)
        q_mid = q * jnp.exp(bc - bmid)
        k_mid = (k * jnp.exp(bmid - bc)).astype(BF16)
        k_end = (k * jnp.exp(blast - bc)).astype(BF16)
        lhs = jnp.concatenate(
            [jnp.where(lane_k == h, q_mid, 0.0).astype(BF16) for h in range(GLA_HEADS)], axis=0)
        attn = jnp.where(causal4, _mm_nt(lhs, k_mid), 0.0).astype(BF16)
        o_inter = _mm_nt(q_dec, gla_state.astype(BF16))
        drain(tasks, per_chunk // 2)
        for h in range(GLA_HEADS):
            hs = slice(h * GLA_DV, (h + 1) * GLA_DV)
            o = _mm(attn[h * CHUNK:(h + 1) * CHUNK], v[:, hs]) + o_inter[:, hs]
            brb_scr[rs, hs] = o * lax.rsqrt(jnp.mean(o * o, axis=-1, keepdims=True) + EPS)
        gla_state = jnp.where(st_blockdiag, gla_state * jnp.exp(blast) + _mm_tn(v, k_end), 0.0)
        drain(tasks, per_chunk - per_chunk // 2)
    drain(tasks, len(tasks))
    gla_scr[...] = gla_state
    yb_scr[...] = (brb_scr[...] * prm(P_GNORM) * sgg_scr[...]).astype(BF16)

    tasks = [merge_tile(yb_scr, 1, gate_b_scr, t * MXU_TILE, False) for t in range(4)]
    tasks += [proj_tile(W_M + 2 * D_MODEL + t * MXU_TILE, gate_ac_scr, t * MXU_TILE, jnp.tanh)
              for t in range(4)]
    per_chunk = -(-len(tasks) // nchunk)

    emat = emat_ref[...]
    tril = (lax.broadcasted_iota(jnp.int32, (CHUNK, CHUNK), 0)
            >= lax.broadcasted_iota(jnp.int32, (CHUNK, CHUNK), 1))
    gw = SSD_D_INNER // SSD_GROUPS
    ssd_state = ssd_scr[...]
    nslab = conv_scr.shape[0]
    row = lambda r, sl: jnp.broadcast_to(prm_ref[r:r + 1, sl * LANES:(sl + 1) * LANES] * 0.5, (8, LANES))
    half_convb = [row(P_CONVB[0], sl) for sl in range(nslab)]
    half_convw = [[row(P_CONVW[0] + i, sl) for i in range(SSD_CONV)] for sl in range(nslab)]
    for c in range(nchunk):
        rs = slice(c * CHUNK, (c + 1) * CHUNK)
        for sl in range(nslab):
            for grp in range(CHUNK // CONV_GROUP):
                base = CONV_PAD + c * CHUNK + grp * CONV_GROUP
                taps = [conv_scr[sl, pl.ds(base + m, 8, stride=CONV_STRIDE), :]
                        for m in range(1 - SSD_CONV, CONV_STRIDE)]
                for m in range(CONV_STRIDE):
                    acc = half_convb[sl]
                    for i in range(SSD_CONV):
                        acc = acc + half_convw[sl][i] * taps[m + i]
                    xbc_scr[sl, pl.ds(c * CHUNK + grp * CONV_GROUP + m, 8, stride=CONV_STRIDE), :] = (
                        _silu_of_double(acc))
        nx = SSD_D_INNER // LANES
        xs = jnp.concatenate([xbc_scr[sl, rs, :] for sl in range(nx)], axis=1)
        bmat = [xbc_scr[nx + g, rs, :].astype(BF16) for g in range(SSD_GROUPS)]
        cmat = [xbc_scr[nx + SSD_GROUPS + g, rs, :] for g in range(SSD_GROUPS)]
        a = acs[rs]
        a_last = a[CHUNK - 1:CHUNK]
        w = dt[rs] * jnp.exp(a_last - a)
        w_hi = w.astype(BF16)
        w_lo = (w - w_hi.astype(F32)).astype(BF16)
        xw = (xs * (_mm(w_hi, emat) + _mm(w_lo, emat))).astype(BF16)
        al3 = _split3(jnp.broadcast_to(a_last, (8, LANES)))
        chunk_decay = jnp.exp(_mm(al3[0], emat) + _mm(al3[1], emat) + _mm(al3[2], emat))[0:1]
        a_t = a.T
        dt_t = dt[rs].T
        x_c = xs.astype(BF16)
        state_b = ssd_state.astype(BF16)
        drain(tasks, per_chunk // 2)
        ys, new_state = [], []
        for g in range(SSD_GROUPS):
            b_g = bmat[g]
            c_g = cmat[g]
            cb = _mm_nt(c_g.astype(BF16), b_g)
            for hh in range(SSD_HEADS // SSD_GROUPS):
                h = g * (SSD_HEADS // SSD_GROUPS) + hh
                a_col = a[:, h:h + 1]
                decay = jnp.where(tril, jnp.exp(a_col - a_t[h:h + 1, :]), 0.0)
                m_h = cb * decay * dt_t[h:h + 1, :]
                lhs = jnp.concatenate([m_h, c_g * jnp.exp(a_col)], axis=1).astype(BF16)
                ps = slice((h // 2) * LANES, (h // 2 + 1) * LANES)
                rhs = jnp.concatenate([x_c[:, ps], state_b[:, ps]], axis=0)
                ys.append(_mm(lhs, rhs))
            ws = slice(g * gw, (g + 1) * gw)
            new_state.append(ssd_state[:, ws] * chunk_decay[:, ws] + _mm_tn(b_g, xw[:, ws]))
        ssd_state = jnp.concatenate(new_state, axis=1)
        y = jnp.concatenate(
            [jnp.where(low_half, ys[2 * p], ys[2 * p + 1]) for p in range(SSD_HEADS // 2)], axis=1)
        u = (y + xs * prm(P_DEXP)) * sz_scr[rs, :]
        normed = []
        for g in range(SSD_GROUPS):
            ug = u[:, g * gw:(g + 1) * gw]
            normed.append(ug * lax.rsqrt(jnp.mean(ug * ug, axis=-1, keepdims=True) + EPS))
        brc_scr[rs, :] = (jnp.concatenate(normed, axis=1) * prm(P_SNORM)).astype(BF16)
        drain(tasks, per_chunk - per_chunk // 2)
    drain(tasks, len(tasks))
    ssd_scr[...] = ssd_state
    conv_scr[:, 0:CONV_PAD, :] = conv_scr[:, ts:ts + CONV_PAD, :]

    u_c = _mm(brc_scr[...], wbr_ref[2])
    merged = merged_scr[...] + (gate_ac_scr[...] * u_c + u_c)
    out_ref[0] = x_ref[0] + _mm(merged.astype(BF16), wo_ref[...])


def _to_ya_lanes(w, axis):
    shape = w.shape
    w = w.reshape(*shape[:axis], ATTN_KV_HEADS, 4, ATTN_HEAD_DIM, *shape[axis + 1:])
    return jnp.swapaxes(w, axis, axis + 1).reshape(shape)


def _const_tables():
    lane = np.arange(LANES)
    gmat = ((lane[:, None] // 32) % 2 == (lane[None, :] // 32) % 2).astype(np.float32) / ATTN_HEAD_DIM
    r = np.arange(2 * CHUNK)
    ltri = ((r[:, None] // CHUNK == r[None, :] // CHUNK) & (r[:, None] >= r[None, :])).astype(np.float32)
    emat = np.zeros((LANES, SSD_D_INNER), np.float32)
    for h in range(SSD_HEADS):
        emat[h, h * SSD_HEAD_DIM:(h + 1) * SSD_HEAD_DIM] = 1.0
    return (jnp.asarray(gmat, BF16), jnp.asarray(ltri, BF16), jnp.asarray(emat, BF16))


def _weight_prep_kernel(wt_ref, o_ref):
    c = IN_PROJ_CUTS

    def put(dst, srcs, width, scale=None):
        pieces = [wt_ref[s:s + width, :] for s in srcs]
        v = pieces[0] if len(pieces) == 1 else jnp.concatenate(pieces, axis=0)
        for k in range(v.shape[0] // LANES):
            t = v[k * LANES:(k + 1) * LANES, :].T
            o_ref[:, dst + k * LANES:dst + (k + 1) * LANES] = (t if scale is None else t * scale).astype(BF16)

    for blk in range(4):
        put(W_Q + blk * LANES, [c[0] + (blk + 4 * (r % 2)) * 64 + 32 * (r // 2) for r in range(4)], 32)
        put(W_GA + blk * LANES, [c[3] + (blk + 4 * g) * 64 for g in range(2)], 64, 0.5)
    put(W_KV, [c[1] + (r % 2) * 64 + 32 * (r // 2) for r in range(4)], 32)
    put(W_KV + LANES, [c[2]], LANES)
    put(W_GQK, [c[4]], 512)
    put(W_GV, [c[6]], 512)
    put(W_GG, [c[7]], 512, 0.5)
    small = jnp.concatenate([wt_ref[c[10]:c[11], :], wt_ref[c[8]:c[9], :],
                             jnp.zeros((LANES - 24, wt_ref.shape[1]), F32)], axis=0)
    o_ref[:, W_SM:W_SM + LANES] = small.T.astype(BF16)
    put(W_XBC, [c[9]], 1024)
    put(W_Z, [c[11]], 512, 0.5)
    put(W_M, [c[12]], 3 * D_MODEL, 0.5)


def _prep_in_proj(w_in):
    depth, d_in, n = w_in.shape
    return pl.pallas_call(
        _weight_prep_kernel,
        grid=(depth, d_in // PREP_ROWS),
        in_specs=[pl.BlockSpec((None, n, PREP_ROWS), lambda l, r: (l, 0, r))],
        out_specs=pl.BlockSpec((None, PREP_ROWS, W_END), lambda l, r: (l, r, 0)),
        out_shape=jax.ShapeDtypeStruct((depth, d_in, W_END), BF16),
        compiler_params=pltpu.CompilerParams(vmem_limit_bytes=VMEM_LIMIT_BYTES),
        name="weight_prep",
    )(jnp.swapaxes(w_in, 1, 2))


def _prep_params(norm_g, w_in, attn_q_norm, attn_k_norm, gla_w_gate_up, gla_b_gate, gla_out_norm,
                 ssd_conv_w, ssd_conv_b, ssd_dt_bias, ssd_A_log, ssd_D, ssd_out_norm, w_branch, w_out):
    depth = w_in.shape[0]
    w_cat = _prep_in_proj(w_in)
    wbr = (0.5 * jnp.concatenate([_to_ya_lanes(w_branch[:, 0:1], 2), w_branch[:, 1:]], axis=1)).astype(BF16)
    wup = jnp.zeros((depth, LANES, gla_w_gate_up.shape[-1]), F32).at[:, 8:8 + GLA_GATE_RANK].set(
        gla_w_gate_up).astype(BF16)
    lane_gain = lambda g: jnp.repeat(g.reshape(depth, 2, 1, 32), 2, axis=2).reshape(depth, LANES)
    pad = lambda v, n: jnp.pad(v.astype(F32), ((0, 0), (0, n - v.shape[-1])))
    prm = jnp.zeros((depth, P_ROWS, D_MODEL), F32)
    prm = prm.at[:, P_NORM[0]].set(norm_g)
    prm = prm.at[:, P_CONVB[0]].set(ssd_conv_b)
    prm = prm.at[:, P_CONVW[0]:P_CONVW[0] + SSD_CONV].set(ssd_conv_w)
    prm = prm.at[:, P_QGAIN[0]].set(jnp.concatenate([
        lane_gain(attn_q_norm), lane_gain(attn_k_norm), pad(ssd_dt_bias, LANES), pad(ssd_A_log, LANES),
        gla_b_gate, jnp.zeros((depth, D_MODEL - 768), F32)], axis=-1))
    prm = prm.at[:, P_GNORM[0]].set(jnp.concatenate([
        jnp.tile(gla_out_norm, (1, GLA_HEADS)), jnp.repeat(ssd_D, SSD_HEAD_DIM, axis=-1)], axis=-1))
    prm = prm.at[:, P_SNORM[0], :SSD_D_INNER].set(ssd_out_norm)
    return w_cat, wbr, w_out.astype(BF16), wup, prm


def _layer_call(x, cos_t, sin_t, sinks, params, consts, layer, ts):
    bsz, seq, d = x.shape
    tok = lambda w: pl.BlockSpec((1, ts, w), lambda b, j: (b, j, 0))

    def layer_spec(arr):
        nd = arr.ndim - 1
        return pl.BlockSpec((None,) + arr.shape[1:], lambda b, j, _nd=nd: (layer,) + (0,) * _nd,
                            pipeline_mode=pl.Buffered(1))

    def const_spec(arr):
        return pl.BlockSpec(arr.shape, lambda b, j, _nd=arr.ndim: (0,) * _nd, pipeline_mode=pl.Buffered(1))

    rope_spec = pl.BlockSpec((1, ts // ROPE_TOKENS, LANES), lambda b, j: (b, j, 0))
    in_specs = [pl.BlockSpec(memory_space=pltpu.SMEM), tok(d), rope_spec, rope_spec]
    in_specs += [layer_spec(a) for a in params] + [const_spec(a) for a in consts]
    conv_slabs = (SSD_D_INNER + 2 * SSD_GROUPS * SSD_D_STATE) // LANES
    scratch = [
        pltpu.VMEM((ts, d), BF16),
        pltpu.VMEM((ATTN_KV_HEADS, 4 * ts, LANES), BF16),
        pltpu.VMEM((ts + ATTN_BLOCK, LANES), BF16),
        pltpu.VMEM((ts + ATTN_BLOCK, LANES), BF16),
        pltpu.VMEM((ts, 512), F32),
        pltpu.VMEM((ts, 512), F32),
        pltpu.VMEM((ts, 512), BF16),
        pltpu.VMEM((ts, d), F32),
        pltpu.VMEM((conv_slabs, ts + CONV_PAD, LANES), F32),
        pltpu.VMEM((conv_slabs, ts, LANES), F32),
        pltpu.VMEM((GLA_HEADS * GLA_DV, GLA_HEADS * GLA_DK), F32),
        pltpu.VMEM((SSD_D_STATE, SSD_D_INNER), F32),
        pltpu.VMEM((ts, 512), F32),
        pltpu.VMEM((ts, 512), BF16),
        pltpu.VMEM((ts, LANES), F32),
        pltpu.VMEM((ts, 512), F32),
        pltpu.VMEM((ts, 512), F32),
        pltpu.VMEM((ts, 512), F32),
        pltpu.VMEM((ts, d), F32),
        pltpu.VMEM((ts, d), F32),
        pltpu.VMEM((ts, 512), BF16),
        pltpu.VMEM((ts, 512), BF16),
        pltpu.VMEM((ts, LANES), F32),
        pltpu.VMEM((ts, LANES), F32),
    ]
    return pl.pallas_call(
        functools.partial(_layer_kernel, ts=ts, layer=layer),
        grid=(bsz, seq // ts),
        in_specs=in_specs,
        out_specs=pl.BlockSpec((1, ts, d), lambda b, j: (b, j, 0)),
        out_shape=jax.ShapeDtypeStruct(x.shape, x.dtype),
        scratch_shapes=scratch,
        compiler_params=pltpu.CompilerParams(
            dimension_semantics=("arbitrary", "arbitrary"),
            vmem_limit_bytes=VMEM_LIMIT_BYTES),
        name="hybrid_layer",
    )(sinks.astype(F32), x, cos_t, sin_t, *params, *consts)


def _rope_tables(positions):
    bsz, seq = positions.shape
    nfreq = ATTN_HEAD_DIM // 2
    per_row = LANES // nfreq
    inv_freq = ROPE_THETA ** (-jnp.arange(0, ATTN_HEAD_DIM, 2, dtype=F32) / ATTN_HEAD_DIM)
    freq = jnp.tile(inv_freq, per_row).reshape(1, LANES)
    pos = jnp.repeat(positions.reshape(bsz, seq // per_row, per_row), nfreq, axis=2)
    spec = pl.BlockSpec((1, seq // per_row, LANES), lambda b: (b, 0, 0))
    return pl.pallas_call(
        _rope_table_kernel,
        grid=(bsz,),
        in_specs=[spec, pl.BlockSpec((1, LANES), lambda b: (0, 0))],
        out_specs=[spec, spec],
        out_shape=[jax.ShapeDtypeStruct((bsz, seq // per_row, LANES), F32)] * 2,
        name="rope_tables",
    )(pos, freq)


def kernel(x, positions, norm_g, w_in, attn_q_norm, attn_k_norm, attn_sinks, gla_w_gate_up, gla_b_gate,
           gla_out_norm, ssd_conv_w, ssd_conv_b, ssd_dt_bias, ssd_A_log, ssd_D, ssd_out_norm,
           w_branch, w_out):
    seq = x.shape[1]
    ts = min(SEQ_BLOCK, seq)
    assert seq % ts == 0 and ts % (2 * CHUNK) == 0 and x.shape[2] == D_MODEL
    cos_t, sin_t = _rope_tables(positions)
    consts = _const_tables()
    params = _prep_params(norm_g, w_in, attn_q_norm, attn_k_norm, gla_w_gate_up, gla_b_gate, gla_out_norm,
                          ssd_conv_w, ssd_conv_b, ssd_dt_bias, ssd_A_log, ssd_D, ssd_out_norm,
                          w_branch, w_out)
    for layer in range(w_in.shape[0]):
        x = _layer_call(x, cos_t, sin_t, attn_sinks, params, consts, layer, ts)
    return x
```

```python
import functools

import numpy as np
import jax
import jax.numpy as jnp
from jax import lax
from jax.experimental import pallas as pl
from jax.experimental.pallas import tpu as pltpu

F32 = jnp.float32
BF16 = jnp.bfloat16

D_MODEL = 1024
EPS = 1e-6
ROPE_THETA = 10000.0

ATTN_HEADS = 8
ATTN_KV_HEADS = 2
ATTN_HEAD_DIM = 64
ATTN_BLOCK = 128
GLA_HEADS = 4
GLA_DK = 64
GLA_DV = 128
GLA_GATE_RANK = 16
GLA_GATE_NORMALIZER = 16.0
SSD_HEADS = 8
SSD_HEAD_DIM = 64
SSD_GROUPS = 2
SSD_D_STATE = 128
SSD_D_INNER = 512
SSD_CONV = 4

IN_PROJ_SIZES = (512, 128, 128, 512, 256, 256, 512, 512, 16, 1024, 8, 512, 3072)
IN_PROJ_CUTS = tuple(int(v) for v in np.cumsum((0,) + IN_PROJ_SIZES))

LANES = 128
CHUNK = 128
CONV_PAD = 8
SEQ_BLOCK = 512
PREP_ROWS = 256
VMEM_LIMIT_BYTES = 56 * 1024 * 1024
MASK_VALUE = -1e30
MXU_TILE = 256
SOFTMAX_ROWS = 64
ROPE_TOKENS = 4
CONV_STRIDE = 4
CONV_GROUP = 8 * CONV_STRIDE

W_Q, W_KV, W_GA, W_GQK, W_GV, W_GG, W_SM, W_XBC, W_Z, W_M, W_END = (
    0, 512, 768, 1280, 1792, 2304, 2816, 2944, 3968, 4480, 7552)
P_NORM, P_CONVB, P_CONVW = (0, 0, 1024), (1, 0, 1024), (2, 0, 1024)
P_QGAIN, P_KGAIN, P_DTB, P_ALOG, P_BGATE = (6, 0, 128), (6, 128, 128), (6, 256, 128), (6, 384, 128), (6, 512, 256)
P_GNORM, P_DEXP = (7, 0, 512), (7, 512, 512)
P_SNORM = (8, 0, 512)
P_ROWS = 16


def _mm(a, b):
    return jnp.dot(a, b, preferred_element_type=F32)


def _mm_nt(a, b):
    return lax.dot_general(a, b, (((1,), (1,)), ((), ())), preferred_element_type=F32)


def _mm_tn(a, b):
    return lax.dot_general(a, b, (((0,), (0,)), ((), ())), preferred_element_type=F32)


def _silu_of_double(vh):
    return vh * jnp.tanh(vh) + vh


def _softplus(v):
    return jnp.maximum(v, 0.0) + jnp.log(1.0 + jnp.exp(-jnp.abs(v)))


def _split3(v):
    hi = v.astype(BF16)
    r1 = v - hi.astype(F32)
    mid = r1.astype(BF16)
    lo = (r1 - mid.astype(F32)).astype(BF16)
    return hi, mid, lo


def _rope_table_kernel(pos_ref, freq_ref, cos_ref, sin_ref):
    ang = pos_ref[0].astype(F32) * freq_ref[...]
    cos_ref[0] = jnp.cos(ang)
    sin_ref[0] = jnp.sin(ang)


def _layer_kernel(
        sinks_ref, x_ref, cos_ref, sin_ref, w_ref, wbr_ref, wo_ref, wup_ref, prm_ref,
        gmat_ref, ltri_ref, emat_ref,
        out_ref,
        hb_scr, q_scr, k_scr, v_scr, bra_scr, brb_scr, brc_scr, merged_scr, conv_scr, xbc_scr,
        gla_scr, ssd_scr, gqk_scr, gv_scr, psm_scr, sga_scr, sgg_scr, sz_scr, gate_ac_scr, gate_b_scr,
        ya_scr, yb_scr, cos_scr, sin_scr,
        *, ts, layer):
    nblk = ts // ATTN_BLOCK
    nchunk = ts // CHUNK
    j = pl.program_id(1)

    @pl.when(j == 0)
    def _reset_carries():
        k_scr[0:ATTN_BLOCK, :] = jnp.zeros((ATTN_BLOCK, LANES), BF16)
        v_scr[0:ATTN_BLOCK, :] = jnp.zeros((ATTN_BLOCK, LANES), BF16)
        conv_scr[:, 0:CONV_PAD, :] = jnp.zeros((conv_scr.shape[0], CONV_PAD, LANES), F32)
        gla_scr[...] = jnp.zeros(gla_scr.shape, F32)
        ssd_scr[...] = jnp.zeros(ssd_scr.shape, F32)

    def prm(p):
        return prm_ref[p[0]:p[0] + 1, p[1]:p[1] + p[2]]

    def proj_tile(c0, dst_ref, d0, fn=None, width=MXU_TILE):
        def run():
            t = _mm(hb_scr[...], w_ref[:, c0:c0 + width])
            dst_ref[:, d0:d0 + width] = (t if fn is None else fn(t)).astype(dst_ref.dtype)
        return run

    def conv_in_tile(c0):
        def run():
            t = _mm(hb_scr[...], w_ref[:, W_XBC + c0:W_XBC + c0 + MXU_TILE])
            for i in range(MXU_TILE // LANES):
                conv_scr[c0 // LANES + i, CONV_PAD:CONV_PAD + ts, :] = t[:, i * LANES:(i + 1) * LANES]
        return run

    def merge_tile(y_ref, branch, gate_ref, c0, first):
        def run():
            u = _mm(y_ref[...], wbr_ref[branch, :, c0:c0 + MXU_TILE])
            g = gate_ref[:, c0:c0 + MXU_TILE] * u + u
            if first:
                merged_scr[:, c0:c0 + MXU_TILE] = g
            else:
                merged_scr[:, c0:c0 + MXU_TILE] += g
        return run

    def drain(tasks, count):
        for _ in range(min(count, len(tasks))):
            tasks.pop(0)()

    half = ts // 4
    for r in range(4):
        x = x_ref[0, r * half:(r + 1) * half, :]
        ms = jnp.mean(x * x, axis=-1, keepdims=True)
        hb_scr[r * half:(r + 1) * half, :] = (x * lax.rsqrt(ms + EPS) * prm(P_NORM)).astype(BF16)

    lane = lax.broadcasted_iota(jnp.int32, (ATTN_BLOCK, LANES), 1)
    low_half = lane < LANES // 2

    lane_q = lax.broadcasted_iota(jnp.int32, (ts // ROPE_TOKENS, LANES), 1) // (LANES // ROPE_TOKENS)
    for src_ref, dst_scr in ((cos_ref, cos_scr), (sin_ref, sin_scr)):
        compact = src_ref[0]
        for p in range(ROPE_TOKENS):
            one = jnp.where(lane_q == p, compact, 0.0)
            two = one + pltpu.roll(one, LANES // 4, 1)
            dst_scr[pl.ds(p, ts // ROPE_TOKENS, stride=ROPE_TOKENS), :] = two + pltpu.roll(two, LANES // 2, 1)
    lane_ts = lax.broadcasted_iota(jnp.int32, (ts, LANES), 1)
    cos_t = cos_scr[...]
    sin_t = jnp.where(lane_ts < LANES // 2, -sin_scr[...], sin_scr[...])
    gmat = gmat_ref[...]
    akv = jnp.concatenate([_mm(hb_scr[r * half:(r + 1) * half, :], w_ref[:, W_KV:W_GA]) for r in range(4)],
                          axis=0)
    ak = akv[:, :LANES]
    kn = ak * lax.rsqrt(_mm((ak * ak).astype(BF16), gmat) + EPS) * prm(P_KGAIN)
    kr = kn * cos_t + pltpu.roll(kn, LANES // 2, 1) * sin_t
    k_scr[ATTN_BLOCK:ATTN_BLOCK + ts, :] = kr.astype(BF16)
    v_scr[ATTN_BLOCK:ATTN_BLOCK + ts, :] = akv[:, LANES:].astype(BF16)

    aq = jnp.concatenate([_mm(hb_scr[r * half:(r + 1) * half, :], w_ref[:, W_Q:W_KV]) for r in range(4)],
                         axis=0)
    group0 = (lane_ts // 32) % 2 == 0
    qgain = prm(P_QGAIN) * (ATTN_HEAD_DIM ** -0.5)
    for c in range(4):
        qc = aq[:, c * LANES:(c + 1) * LANES]
        qn = qc * lax.rsqrt(_mm((qc * qc).astype(BF16), gmat) + EPS) * qgain
        qr = qn * cos_t + pltpu.roll(qn, LANES // 2, 1) * sin_t
        q_scr[0, c * ts:(c + 1) * ts, :] = jnp.where(group0, qr, 0.0).astype(BF16)
        q_scr[1, c * ts:(c + 1) * ts, :] = jnp.where(group0, 0.0, qr).astype(BF16)

    tasks = [proj_tile(W_SM, psm_scr, 0, width=LANES)]
    tasks += [proj_tile(W_GQK + t * MXU_TILE, gqk_scr, t * MXU_TILE) for t in range(2)]
    tasks += [proj_tile(W_GV + t * MXU_TILE, gv_scr, t * MXU_TILE) for t in range(2)]
    tasks += [conv_in_tile(t * MXU_TILE) for t in range(4)]
    drain(tasks, len(tasks))
    tasks = [proj_tile(W_GA + t * MXU_TILE, sga_scr, t * MXU_TILE, _silu_of_double) for t in range(2)]
    tasks += [proj_tile(W_GG + t * MXU_TILE, sgg_scr, t * MXU_TILE, _silu_of_double) for t in range(2)]
    tasks += [proj_tile(W_Z + t * MXU_TILE, sz_scr, t * MXU_TILE, _silu_of_double) for t in range(2)]
    tasks += [proj_tile(W_M + t * MXU_TILE, gate_ac_scr, t * MXU_TILE, jnp.tanh) for t in range(4)]
    per_blk = -(-len(tasks) // nblk)

    qi = lax.broadcasted_iota(jnp.int32, (ATTN_BLOCK, ATTN_BLOCK), 0)
    ki = lax.broadcasted_iota(jnp.int32, (ATTN_BLOCK, ATTN_BLOCK), 1)
    from_prev = ki > qi
    first_block_bias = jnp.where(j > 0, 0.0, MASK_VALUE)
    for n in range(nblk):
        kb = k_scr[n * ATTN_BLOCK:(n + 2) * ATTN_BLOCK, :]
        vb = v_scr[n * ATTN_BLOCK:(n + 2) * ATTN_BLOCK, :]
        outs = []
        for g in range(ATTN_KV_HEADS):
            lhs = jnp.concatenate(
                [q_scr[g, c * ts + n * ATTN_BLOCK:c * ts + (n + 1) * ATTN_BLOCK, :] for c in range(4)],
                axis=0)
            s_all = _mm_nt(lhs, kb)
            probs, scales = [], []
            for r in range(4 * ATTN_BLOCK // SOFTMAX_ROWS):
                r0 = r * SOFTMAX_ROWS
                q0 = r0 % ATTN_BLOCK
                sink = sinks_ref[layer, 4 * g + r0 // ATTN_BLOCK]
                s_prev = s_all[r0:r0 + SOFTMAX_ROWS, :ATTN_BLOCK]
                if n == 0:
                    s_prev = s_prev + first_block_bias
                prev_sel = from_prev[q0:q0 + SOFTMAX_ROWS]
                s = jnp.where(prev_sel, s_prev, s_all[r0:r0 + SOFTMAX_ROWS, ATTN_BLOCK:])
                m = jnp.maximum(jnp.max(s, axis=1, keepdims=True), sink)
                p = jnp.exp(s - m)
                denom = jnp.sum(p, axis=1, keepdims=True) + jnp.exp(sink - m)
                pb = p.astype(BF16)
                zero = jnp.zeros_like(pb)
                probs.append(jnp.concatenate(
                    [jnp.where(prev_sel, pb, zero), jnp.where(prev_sel, zero, pb)], axis=1))
                scales.append(1.0 / denom)
            outs.append(_mm(jnp.concatenate(probs, axis=0), vb) * jnp.concatenate(scales, axis=0))
            drain(tasks, (per_blk + 1 - g) // 2)
        for c in range(4):
            bra_scr[n * ATTN_BLOCK:(n + 1) * ATTN_BLOCK, c * LANES:(c + 1) * LANES] = jnp.where(
                low_half,
                outs[0][c * ATTN_BLOCK:(c + 1) * ATTN_BLOCK],
                outs[1][c * ATTN_BLOCK:(c + 1) * ATTN_BLOCK])
    drain(tasks, len(tasks))
    k_scr[0:ATTN_BLOCK, :] = k_scr[ts:ts + ATTN_BLOCK, :]
    v_scr[0:ATTN_BLOCK, :] = v_scr[ts:ts + ATTN_BLOCK, :]
    ya_scr[...] = (bra_scr[...] * sga_scr[...]).astype(BF16)

    psm = psm_scr[...]
    logit = _mm(psm.astype(BF16), wup_ref[...]) + prm(P_BGATE)
    log_alpha = (jnp.minimum(logit, 0.0) - jnp.log(1.0 + jnp.exp(-jnp.abs(logit)))) * (
        1.0 / GLA_GATE_NORMALIZER)
    dt = _softplus(psm + prm(P_DTB))
    adt = dt * (-jnp.exp(prm(P_ALOG)))
    cs_in = jnp.concatenate([log_alpha, adt], axis=1)
    ltri = ltri_ref[...]
    cs_rows = ltri.shape[0]
    cs_parts = []
    for r in range(ts // cs_rows):
        pieces = _split3(cs_in[r * cs_rows:(r + 1) * cs_rows])
        cs_parts.append(_mm(ltri, pieces[0]) + _mm(ltri, pieces[1]) + _mm(ltri, pieces[2]))
    cs = cs_parts[0] if len(cs_parts) == 1 else jnp.concatenate(cs_parts, axis=0)
    bcum = cs[:, :GLA_HEADS * GLA_DK]
    acs = cs[:, GLA_HEADS * GLA_DK:]

    tasks = [merge_tile(ya_scr, 0, gate_ac_scr, t * MXU_TILE, True) for t in range(4)]
    tasks += [proj_tile(W_M + D_MODEL + t * MXU_TILE, gate_b_scr, t * MXU_TILE, jnp.tanh)
              for t in range(4)]
    per_chunk = -(-len(tasks) // nchunk)

    dkw = GLA_HEADS * GLA_DK
    lane_k = lax.broadcasted_iota(jnp.int32, (CHUNK, dkw), 1) // GLA_DK
    ci = lax.broadcasted_iota(jnp.int32, (GLA_HEADS * CHUNK, CHUNK), 0) % CHUNK
    cj = lax.broadcasted_iota(jnp.int32, (GLA_HEADS * CHUNK, CHUNK), 1)
    causal4 = ci >= cj
    st_blockdiag = (lax.broadcasted_iota(jnp.int32, gla_scr.shape, 0) // GLA_DV
                    == lax.broadcasted_iota(jnp.int32, gla_scr.shape, 1) // GLA_DK)
    gla_state = gla_scr[...]
    for c in range(nchunk):
        rs = slice(c * CHUNK, (c + 1) * CHUNK)
        bc = bcum[rs]
        bmid = bc[CHUNK // 2 - 1:CHUNK // 2]
        blast = bc[CHUNK - 1:CHUNK]
        q = gqk_scr[rs, :dkw] * (GLA_DK ** -0.5)
        k = gqk_scr[rs, dkw:]
        v = gv_scr[rs, :]
        q_dec = (q * jnp.exp(bc)).astype(BF16)
        q_mid = q * jnp.exp(bc - bmid)
        k_mid = (k * jnp.exp(bmid - bc)).astype(BF16)
        k_end = (k * jnp.exp(blast - bc)).astype(BF16)
        attn = jnp.concatenate(
            [jnp.where(causal4[:CHUNK], _mm_nt(jnp.where(lane_k == h, q_mid, 0.0).astype(BF16), k_mid), 0.0)
             for h in range(GLA_HEADS)], axis=0).astype(BF16)
        o_inter = _mm_nt(q_dec, gla_state.astype(BF16))
        drain(tasks, per_chunk // 2)
        for h in range(GLA_HEADS):
            hs = slice(h * GLA_DV, (h + 1) * GLA_DV)
            o = _mm(attn[h * CHUNK:(h + 1) * CHUNK], v[:, hs]) + o_inter[:, hs]
            brb_scr[rs, hs] = o * lax.rsqrt(jnp.mean(o * o, axis=-1, keepdims=True) + EPS)
        gla_state = jnp.where(st_blockdiag, gla_state * jnp.exp(blast) + _mm_tn(v, k_end), 0.0)
        drain(tasks, per_chunk - per_chunk // 2)
    drain(tasks, len(tasks))
    gla_scr[...] = gla_state
    yb_scr[...] = (brb_scr[...] * prm(P_GNORM) * sgg_scr[...]).astype(BF16)

    tasks = [merge_tile(yb_scr, 1, gate_b_scr, t * MXU_TILE, False) for t in range(4)]
    tasks += [proj_tile(W_M + 2 * D_MODEL + t * MXU_TILE, gate_ac_scr, t * MXU_TILE, jnp.tanh)
              for t in range(4)]
    per_chunk = -(-len(tasks) // nchunk)

    emat = emat_ref[...]
    tril = (lax.broadcasted_iota(jnp.int32, (CHUNK, CHUNK), 0)
            >= lax.broadcasted_iota(jnp.int32, (CHUNK, CHUNK), 1))
    gw = SSD_D_INNER // SSD_GROUPS
    ssd_state = ssd_scr[...]
    a_last_rows = jnp.concatenate(
        [jnp.broadcast_to(acs[(c + 1) * CHUNK - 1:(c + 1) * CHUNK], (CHUNK, LANES)) for c in range(nchunk)], axis=0)
    w_all = dt * jnp.exp(a_last_rows - acs)
    w_hi = w_all.astype(BF16)
    w_lo = (w_all - w_hi.astype(F32)).astype(BF16)
    w_exp = _mm(w_hi, emat) + _mm(w_lo, emat)
    al3 = _split3(jnp.concatenate(
        [jnp.broadcast_to(acs[(c + 1) * CHUNK - 1:(c + 1) * CHUNK], (8, LANES)) for c in range(nchunk)], axis=0))
    chunk_decays = jnp.exp(_mm(al3[0], emat) + _mm(al3[1], emat) + _mm(al3[2], emat))
    nslab = conv_scr.shape[0]
    row = lambda r, sl: jnp.broadcast_to(prm_ref[r:r + 1, sl * LANES:(sl + 1) * LANES] * 0.5, (8, LANES))
    half_convb = [row(P_CONVB[0], sl) for sl in range(nslab)]
    half_convw = [[row(P_CONVW[0] + i, sl) for i in range(SSD_CONV)] for sl in range(nslab)]
    for c in range(nchunk):
        rs = slice(c * CHUNK, (c + 1) * CHUNK)
        for sl in range(nslab):
            for grp in range(CHUNK // CONV_GROUP):
                base = CONV_PAD + c * CHUNK + grp * CONV_GROUP
                taps = [conv_scr[sl, pl.ds(base + m, 8, stride=CONV_STRIDE), :]
                        for m in range(1 - SSD_CONV, CONV_STRIDE)]
                for m in range(CONV_STRIDE):
                    acc = half_convb[sl]
                    for i in range(SSD_CONV):
                        acc = acc + half_convw[sl][i] * taps[m + i]
                    xbc_scr[sl, pl.ds(c * CHUNK + grp * CONV_GROUP + m, 8, stride=CONV_STRIDE), :] = (
                        _silu_of_double(acc))
        nx = SSD_D_INNER // LANES
        xs = jnp.concatenate([xbc_scr[sl, rs, :] for sl in range(nx)], axis=1)
        bmat = [xbc_scr[nx + g, rs, :].astype(BF16) for g in range(SSD_GROUPS)]
        cmat = [xbc_scr[nx + SSD_GROUPS + g, rs, :] for g in range(SSD_GROUPS)]
        a = acs[rs]
        xw = (xs * w_exp[rs]).astype(BF16)
        chunk_decay = chunk_decays[8 * c:8 * c + 1]
        a_t = a.T
        dt_t = dt[rs].T
        x_c = xs.astype(BF16)
        state_b = ssd_state.astype(BF16)
        drain(tasks, per_chunk // 2)
        ys, new_state = [], []
        for g in range(SSD_GROUPS):
            b_g = bmat[g]
            c_g = cmat[g]
            cb = _mm_nt(c_g.astype(BF16), b_g)
            for hh in range(SSD_HEADS // SSD_GROUPS):
                h = g * (SSD_HEADS // SSD_GROUPS) + hh
                a_col = a[:, h:h + 1]
                decay = jnp.where(tril, jnp.exp(a_col - a_t[h:h + 1, :]), 0.0)
                m_h = cb * decay * dt_t[h:h + 1, :]
                lhs = jnp.concatenate([m_h, c_g * jnp.exp(a_col)], axis=1).astype(BF16)
                ys.append(lhs)
            ws = slice(g * gw, (g + 1) * gw)
            new_state.append(ssd_state[:, ws] * chunk_decay[:, ws] + _mm_tn(b_g, xw[:, ws]))
        ssd_state = jnp.concatenate(new_state, axis=1)
        pair_out = []
        for p in range(SSD_HEADS // 2):
            ps = slice(p * LANES, (p + 1) * LANES)
            rhs = jnp.concatenate([x_c[:, ps], state_b[:, ps]], axis=0)
            both = _mm(jnp.concatenate([ys[2 * p], ys[2 * p + 1]], axis=0), rhs)
            pair_out.append(jnp.where(low_half, both[:CHUNK], both[CHUNK:]))
        y = jnp.concatenate(pair_out, axis=1)
        u = (y + xs * prm(P_DEXP)) * sz_scr[rs, :]
        normed = []
        for g in range(SSD_GROUPS):
            ug = u[:, g * gw:(g + 1) * gw]
            normed.append(ug * lax.rsqrt(jnp.mean(ug * ug, axis=-1, keepdims=True) + EPS))
        brc_scr[rs, :] = (jnp.concatenate(normed, axis=1) * prm(P_SNORM)).astype(BF16)
        drain(tasks, per_chunk - per_chunk // 2)
    drain(tasks, len(tasks))
    ssd_scr[...] = ssd_state
    conv_scr[:, 0:CONV_PAD, :] = conv_scr[:, ts:ts + CONV_PAD, :]

    u_c = _mm(brc_scr[...], wbr_ref[2])
    merged = merged_scr[...] + (gate_ac_scr[...] * u_c + u_c)
    out_ref[0] = x_ref[0] + _mm(merged.astype(BF16), wo_ref[...])


def _to_ya_lanes(w, axis):
    shape = w.shape
    w = w.reshape(*shape[:axis], ATTN_KV_HEADS, 4, ATTN_HEAD_DIM, *shape[axis + 1:])
    return jnp.swapaxes(w, axis, axis + 1).reshape(shape)


def _const_tables():
    lane = np.arange(LANES)
    gmat = ((lane[:, None] // 32) % 2 == (lane[None, :] // 32) % 2).astype(np.float32) / ATTN_HEAD_DIM
    r = np.arange(2 * CHUNK)
    ltri = ((r[:, None] // CHUNK == r[None, :] // CHUNK) & (r[:, None] >= r[None, :])).astype(np.float32)
    emat = np.zeros((LANES, SSD_D_INNER), np.float32)
    for h in range(SSD_HEADS):
        emat[h, h * SSD_HEAD_DIM:(h + 1) * SSD_HEAD_DIM] = 1.0
    return (jnp.asarray(gmat, BF16), jnp.asarray(ltri, BF16), jnp.asarray(emat, BF16))


def _weight_prep_kernel(wt_ref, o_ref):
    c = IN_PROJ_CUTS

    def put(dst, srcs, width, scale=None):
        pieces = [wt_ref[s:s + width, :] for s in srcs]
        v = pieces[0] if len(pieces) == 1 else jnp.concatenate(pieces, axis=0)
        for k in range(v.shape[0] // LANES):
            t = v[k * LANES:(k + 1) * LANES, :].T
            o_ref[:, dst + k * LANES:dst + (k + 1) * LANES] = (t if scale is None else t * scale).astype(BF16)

    for blk in range(4):
        put(W_Q + blk * LANES, [c[0] + (blk + 4 * (r % 2)) * 64 + 32 * (r // 2) for r in range(4)], 32)
        put(W_GA + blk * LANES, [c[3] + (blk + 4 * g) * 64 for g in range(2)], 64, 0.5)
    put(W_KV, [c[1] + (r % 2) * 64 + 32 * (r // 2) for r in range(4)], 32)
    put(W_KV + LANES, [c[2]], LANES)
    put(W_GQK, [c[4]], 512)
    put(W_GV, [c[6]], 512)
    put(W_GG, [c[7]], 512, 0.5)
    small = jnp.concatenate([wt_ref[c[10]:c[11], :], wt_ref[c[8]:c[9], :],
                             jnp.zeros((LANES - 24, wt_ref.shape[1]), F32)], axis=0)
    o_ref[:, W_SM:W_SM + LANES] = small.T.astype(BF16)
    put(W_XBC, [c[9]], 1024)
    put(W_Z, [c[11]], 512, 0.5)
    put(W_M, [c[12]], 3 * D_MODEL, 0.5)


def _prep_in_proj(w_in):
    depth, d_in, n = w_in.shape
    return pl.pallas_call(
        _weight_prep_kernel,
        grid=(depth, d_in // PREP_ROWS),
        in_specs=[pl.BlockSpec((None, n, PREP_ROWS), lambda l, r: (l, 0, r))],
        out_specs=pl.BlockSpec((None, PREP_ROWS, W_END), lambda l, r: (l, r, 0)),
        out_shape=jax.ShapeDtypeStruct((depth, d_in, W_END), BF16),
        compiler_params=pltpu.CompilerParams(vmem_limit_bytes=VMEM_LIMIT_BYTES),
        name="weight_prep",
    )(jnp.swapaxes(w_in, 1, 2))


def _prep_params(norm_g, w_in, attn_q_norm, attn_k_norm, gla_w_gate_up, gla_b_gate, gla_out_norm,
                 ssd_conv_w, ssd_conv_b, ssd_dt_bias, ssd_A_log, ssd_D, ssd_out_norm, w_branch, w_out):
    depth = w_in.shape[0]
    w_cat = _prep_in_proj(w_in)
    wbr = (0.5 * jnp.concatenate([_to_ya_lanes(w_branch[:, 0:1], 2), w_branch[:, 1:]], axis=1)).astype(BF16)
    wup = jnp.zeros((depth, LANES, gla_w_gate_up.shape[-1]), F32).at[:, 8:8 + GLA_GATE_RANK].set(
        gla_w_gate_up).astype(BF16)
    lane_gain = lambda g: jnp.repeat(g.reshape(depth, 2, 1, 32), 2, axis=2).reshape(depth, LANES)
    pad = lambda v, n: jnp.pad(v.astype(F32), ((0, 0), (0, n - v.shape[-1])))
    prm = jnp.zeros((depth, P_ROWS, D_MODEL), F32)
    prm = prm.at[:, P_NORM[0]].set(norm_g)
    prm = prm.at[:, P_CONVB[0]].set(ssd_conv_b)
    prm = prm.at[:, P_CONVW[0]:P_CONVW[0] + SSD_CONV].set(ssd_conv_w)
    prm = prm.at[:, P_QGAIN[0]].set(jnp.concatenate([
        lane_gain(attn_q_norm), lane_gain(attn_k_norm), pad(ssd_dt_bias, LANES), pad(ssd_A_log, LANES),
        gla_b_gate, jnp.zeros((depth, D_MODEL - 768), F32)], axis=-1))
    prm = prm.at[:, P_GNORM[0]].set(jnp.concatenate([
        jnp.tile(gla_out_norm, (1, GLA_HEADS)), jnp.repeat(ssd_D, SSD_HEAD_DIM, axis=-1)], axis=-1))
    prm = prm.at[:, P_SNORM[0], :SSD_D_INNER].set(ssd_out_norm)
    return w_cat, wbr, w_out.astype(BF16), wup, prm


def _layer_call(x, cos_t, sin_t, sinks, params, consts, layer, ts):
    bsz, seq, d = x.shape
    tok = lambda w: pl.BlockSpec((1, ts, w), lambda b, j: (b, j, 0))

    def layer_spec(arr):
        nd = arr.ndim - 1
        return pl.BlockSpec((None,) + arr.shape[1:], lambda b, j, _nd=nd: (layer,) + (0,) * _nd,
                            pipeline_mode=pl.Buffered(1))

    def const_spec(arr):
        return pl.BlockSpec(arr.shape, lambda b, j, _nd=arr.ndim: (0,) * _nd, pipeline_mode=pl.Buffered(1))

    rope_spec = pl.BlockSpec((1, ts // ROPE_TOKENS, LANES), lambda b, j: (b, j, 0))
    in_specs = [pl.BlockSpec(memory_space=pltpu.SMEM), tok(d), rope_spec, rope_spec]
    in_specs += [layer_spec(a) for a in params] + [const_spec(a) for a in consts]
    conv_slabs = (SSD_D_INNER + 2 * SSD_GROUPS * SSD_D_STATE) // LANES
    scratch = [
        pltpu.VMEM((ts, d), BF16),
        pltpu.VMEM((ATTN_KV_HEADS, 4 * ts, LANES), BF16),
        pltpu.VMEM((ts + ATTN_BLOCK, LANES), BF16),
        pltpu.VMEM((ts + ATTN_BLOCK, LANES), BF16),
        pltpu.VMEM((ts, 512), F32),
        pltpu.VMEM((ts, 512), F32),
        pltpu.VMEM((ts, 512), BF16),
        pltpu.VMEM((ts, d), F32),
        pltpu.VMEM((conv_slabs, ts + CONV_PAD, LANES), F32),
        pltpu.VMEM((conv_slabs, ts, LANES), F32),
        pltpu.VMEM((GLA_HEADS * GLA_DV, GLA_HEADS * GLA_DK), F32),
        pltpu.VMEM((SSD_D_STATE, SSD_D_INNER), F32),
        pltpu.VMEM((ts, 512), F32),
        pltpu.VMEM((ts, 512), BF16),
        pltpu.VMEM((ts, LANES), F32),
        pltpu.VMEM((ts, 512), F32),
        pltpu.VMEM((ts, 512), F32),
        pltpu.VMEM((ts, 512), F32),
        pltpu.VMEM((ts, d), F32),
        pltpu.VMEM((ts, d), F32),
        pltpu.VMEM((ts, 512), BF16),
        pltpu.VMEM((ts, 512), BF16),
        pltpu.VMEM((ts, LANES), F32),
        pltpu.VMEM((ts, LANES), F32),
    ]
    return pl.pallas_call(
        functools.partial(_layer_kernel, ts=ts, layer=layer),
        grid=(bsz, seq // ts),
        in_specs=in_specs,
        out_specs=pl.BlockSpec((1, ts, d), lambda b, j: (b, j, 0)),
        out_shape=jax.ShapeDtypeStruct(x.shape, x.dtype),
        scratch_shapes=scratch,
        compiler_params=pltpu.CompilerParams(
            dimension_semantics=("arbitrary", "arbitrary"),
            vmem_limit_bytes=VMEM_LIMIT_BYTES),
        name="hybrid_layer",
    )(sinks.astype(F32), x, cos_t, sin_t, *params, *consts)


def _rope_tables(positions):
    bsz, seq = positions.shape
    nfreq = ATTN_HEAD_DIM // 2
    per_row = LANES // nfreq
    inv_freq = ROPE_THETA ** (-jnp.arange(0, ATTN_HEAD_DIM, 2, dtype=F32) / ATTN_HEAD_DIM)
    freq = jnp.tile(inv_freq, per_row).reshape(1, LANES)
    pos = jnp.repeat(positions.reshape(bsz, seq // per_row, per_row), nfreq, axis=2)
    spec = pl.BlockSpec((1, seq // per_row, LANES), lambda b: (b, 0, 0))
    return pl.pallas_call(
        _rope_table_kernel,
        grid=(bsz,),
        in_specs=[spec, pl.BlockSpec((1, LANES), lambda b: (0, 0))],
        out_specs=[spec, spec],
        out_shape=[jax.ShapeDtypeStruct((bsz, seq // per_row, LANES), F32)] * 2,
        name="rope_tables",
    )(pos, freq)


def kernel(x, positions, norm_g, w_in, attn_q_norm, attn_k_norm, attn_sinks, gla_w_gate_up, gla_b_gate,
           gla_out_norm, ssd_conv_w, ssd_conv_b, ssd_dt_bias, ssd_A_log, ssd_D, ssd_out_norm,
           w_branch, w_out):
    seq = x.shape[1]
    ts = min(SEQ_BLOCK, seq)
    assert seq % ts == 0 and ts % (2 * CHUNK) == 0 and x.shape[2] == D_MODEL
    cos_t, sin_t = _rope_tables(positions)
    consts = _const_tables()
    params = _prep_params(norm_g, w_in, attn_q_norm, attn_k_norm, gla_w_gate_up, gla_b_gate, gla_out_norm,
                          ssd_conv_w, ssd_conv_b, ssd_dt_bias, ssd_A_log, ssd_D, ssd_out_norm,
                          w_branch, w_out)
    for layer in range(w_in.shape[0]):
        x = _layer_call(x, cos_t, sin_t, attn_sinks, params, consts, layer, ts)
    return x
```

```python
import functools

import numpy as np
import jax
import jax.numpy as jnp
from jax import lax
from jax.experimental import pallas as pl
from jax.experimental.pallas import tpu as pltpu

F32 = jnp.float32
BF16 = jnp.bfloat16

D_MODEL = 1024
EPS = 1e-6
ROPE_THETA = 10000.0

ATTN_HEADS = 8
ATTN_KV_HEADS = 2
ATTN_HEAD_DIM = 64
ATTN_BLOCK = 128
GLA_HEADS = 4
GLA_DK = 64
GLA_DV = 128
GLA_GATE_RANK = 16
GLA_GATE_NORMALIZER = 16.0
SSD_HEADS = 8
SSD_HEAD_DIM = 64
SSD_GROUPS = 2
SSD_D_STATE = 128
SSD_D_INNER = 512
SSD_CONV = 4

IN_PROJ_SIZES = (512, 128, 128, 512, 256, 256, 512, 512, 16, 1024, 8, 512, 3072)
IN_PROJ_CUTS = tuple(int(v) for v in np.cumsum((0,) + IN_PROJ_SIZES))

LANES = 128
CHUNK = 128
CONV_PAD = 8
SEQ_BLOCK = 512
PREP_ROWS = 256
VMEM_LIMIT_BYTES = 56 * 1024 * 1024
MASK_VALUE = -1e30
MXU_TILE = 256
SOFTMAX_ROWS = 64
ROPE_TOKENS = 4
CONV_STRIDE = 4
CONV_GROUP = 8 * CONV_STRIDE

W_Q, W_KV, W_GA, W_GQK, W_GV, W_GG, W_SM, W_XBC, W_Z, W_M, W_END = (
    0, 512, 768, 1280, 1792, 2304, 2816, 2944, 3968, 4480, 7552)
P_NORM, P_CONVB, P_CONVW = (0, 0, 1024), (1, 0, 1024), (2, 0, 1024)
P_QGAIN, P_KGAIN, P_DTB, P_ALOG, P_BGATE = (6, 0, 128), (6, 128, 128), (6, 256, 128), (6, 384, 128), (6, 512, 256)
P_GNORM, P_DEXP = (7, 0, 512), (7, 512, 512)
P_SNORM = (8, 0, 512)
P_ROWS = 16


def _mm(a, b):
    return jnp.dot(a, b, preferred_element_type=F32)


def _mm_nt(a, b):
    return lax.dot_general(a, b, (((1,), (1,)), ((), ())), preferred_element_type=F32)


def _mm_tn(a, b):
    return lax.dot_general(a, b, (((0,), (0,)), ((), ())), preferred_element_type=F32)


def _silu_of_double(vh):
    return vh * jnp.tanh(vh) + vh


def _softplus(v):
    return jnp.maximum(v, 0.0) + jnp.log(1.0 + jnp.exp(-jnp.abs(v)))


def _split3(v):
    hi = v.astype(BF16)
    r1 = v - hi.astype(F32)
    mid = r1.astype(BF16)
    lo = (r1 - mid.astype(F32)).astype(BF16)
    return hi, mid, lo


def _rope_table_kernel(pos_ref, freq_ref, cos_ref, sin_ref):
    ang = pos_ref[0].astype(F32) * freq_ref[...]
    cos_ref[0] = jnp.cos(ang)
    sin_ref[0] = jnp.sin(ang)


def _layer_kernel(
        sinks_ref, x_ref, cos_ref, sin_ref, w_ref, wbr_ref, wo_ref, wup_ref, prm_ref,
        gmat_ref, ltri_ref, emat_ref,
        out_ref,
        hb_scr, q_scr, k_scr, v_scr, bra_scr, brb_scr, brc_scr, merged_scr, conv_scr, xbc_scr,
        gla_scr, ssd_scr, gqk_scr, gv_scr, psm_scr, sga_scr, sgg_scr, sz_scr, gate_ac_scr, gate_b_scr,
        ya_scr, yb_scr, cos_scr, sin_scr,
        *, ts, layer):
    nblk = ts // ATTN_BLOCK
    nchunk = ts // CHUNK
    j = pl.program_id(1)

    @pl.when(j == 0)
    def _reset_carries():
        k_scr[0:ATTN_BLOCK, :] = jnp.zeros((ATTN_BLOCK, LANES), BF16)
        v_scr[0:ATTN_BLOCK, :] = jnp.zeros((ATTN_BLOCK, LANES), BF16)
        conv_scr[:, 0:CONV_PAD, :] = jnp.zeros((conv_scr.shape[0], CONV_PAD, LANES), F32)
        gla_scr[...] = jnp.zeros(gla_scr.shape, F32)
        ssd_scr[...] = jnp.zeros(ssd_scr.shape, F32)

    def prm(p):
        return prm_ref[p[0]:p[0] + 1, p[1]:p[1] + p[2]]

    def proj_tile(c0, dst_ref, d0, fn=None, width=MXU_TILE):
        def run():
            t = _mm(hb_scr[...], w_ref[:, c0:c0 + width])
            dst_ref[:, d0:d0 + width] = (t if fn is None else fn(t)).astype(dst_ref.dtype)
        return run

    def conv_in_tile(c0):
        def run():
            t = _mm(hb_scr[...], w_ref[:, W_XBC + c0:W_XBC + c0 + MXU_TILE])
            for i in range(MXU_TILE // LANES):
                conv_scr[c0 // LANES + i, CONV_PAD:CONV_PAD + ts, :] = t[:, i * LANES:(i + 1) * LANES]
        return run

    def merge_tile(y_ref, branch, gate_ref, c0, first):
        def run():
            u = _mm(y_ref[...], wbr_ref[branch, :, c0:c0 + MXU_TILE])
            g = gate_ref[:, c0:c0 + MXU_TILE] * u + u
            if first:
                merged_scr[:, c0:c0 + MXU_TILE] = g
            else:
                merged_scr[:, c0:c0 + MXU_TILE] += g
        return run

    def drain(tasks, count):
        for _ in range(min(count, len(tasks))):
            tasks.pop(0)()

    half = ts // 4
    for r in range(4):
        x = x_ref[0, r * half:(r + 1) * half, :]
        ms = jnp.mean(x * x, axis=-1, keepdims=True)
        hb_scr[r * half:(r + 1) * half, :] = (x * lax.rsqrt(ms + EPS) * prm(P_NORM)).astype(BF16)

    lane = lax.broadcasted_iota(jnp.int32, (ATTN_BLOCK, LANES), 1)
    low_half = lane < LANES // 2

    lane_q = lax.broadcasted_iota(jnp.int32, (ts // ROPE_TOKENS, LANES), 1) // (LANES // ROPE_TOKENS)
    for src_ref, dst_scr in ((cos_ref, cos_scr), (sin_ref, sin_scr)):
        compact = src_ref[0]
        for p in range(ROPE_TOKENS):
            one = jnp.where(lane_q == p, compact, 0.0)
            two = one + pltpu.roll(one, LANES // 4, 1)
            dst_scr[pl.ds(p, ts // ROPE_TOKENS, stride=ROPE_TOKENS), :] = two + pltpu.roll(two, LANES // 2, 1)
    lane_ts = lax.broadcasted_iota(jnp.int32, (ts, LANES), 1)
    cos_t = cos_scr[...]
    sin_t = jnp.where(lane_ts < LANES // 2, -sin_scr[...], sin_scr[...])
    gmat = gmat_ref[...]
    akv = jnp.concatenate([_mm(hb_scr[r * half:(r + 1) * half, :], w_ref[:, W_KV:W_GA]) for r in range(4)],
                          axis=0)
    ak = akv[:, :LANES]
    kn = ak * lax.rsqrt(_mm((ak * ak).astype(BF16), gmat) + EPS) * prm(P_KGAIN)
    kr = kn * cos_t + pltpu.roll(kn, LANES // 2, 1) * sin_t
    k_scr[ATTN_BLOCK:ATTN_BLOCK + ts, :] = kr.astype(BF16)
    v_scr[ATTN_BLOCK:ATTN_BLOCK + ts, :] = akv[:, LANES:].astype(BF16)

    aq = jnp.concatenate([_mm(hb_scr[r * half:(r + 1) * half, :], w_ref[:, W_Q:W_KV]) for r in range(4)],
                         axis=0)
    group0 = (lane_ts // 32) % 2 == 0
    qgain = prm(P_QGAIN) * (ATTN_HEAD_DIM ** -0.5)
    for c in range(4):
        qc = aq[:, c * LANES:(c + 1) * LANES]
        qn = qc * lax.rsqrt(_mm((qc * qc).astype(BF16), gmat) + EPS) * qgain
        qr = qn * cos_t + pltpu.roll(qn, LANES // 2, 1) * sin_t
        q_scr[0, c * ts:(c + 1) * ts, :] = jnp.where(group0, qr, 0.0).astype(BF16)
        q_scr[1, c * ts:(c + 1) * ts, :] = jnp.where(group0, 0.0, qr).astype(BF16)

    tasks = [proj_tile(W_SM, psm_scr, 0, width=LANES)]
    tasks += [proj_tile(W_GQK + t * MXU_TILE, gqk_scr, t * MXU_TILE) for t in range(2)]
    tasks += [proj_tile(W_GV + t * MXU_TILE, gv_scr, t * MXU_TILE) for t in range(2)]
    tasks += [conv_in_tile(t * MXU_TILE) for t in range(4)]
    drain(tasks, len(tasks))
    tasks = [proj_tile(W_GA + t * MXU_TILE, sga_scr, t * MXU_TILE, _silu_of_double) for t in range(2)]
    tasks += [proj_tile(W_GG + t * MXU_TILE, sgg_scr, t * MXU_TILE, _silu_of_double) for t in range(2)]
    tasks += [proj_tile(W_Z + t * MXU_TILE, sz_scr, t * MXU_TILE, _silu_of_double) for t in range(2)]
    tasks += [proj_tile(W_M + t * MXU_TILE, gate_ac_scr, t * MXU_TILE, jnp.tanh) for t in range(4)]
    per_blk = -(-len(tasks) // nblk)

    qi = lax.broadcasted_iota(jnp.int32, (ATTN_BLOCK, ATTN_BLOCK), 0)
    ki = lax.broadcasted_iota(jnp.int32, (ATTN_BLOCK, ATTN_BLOCK), 1)
    from_prev = ki > qi
    first_block_bias = jnp.where(j > 0, 0.0, MASK_VALUE)
    for n in range(nblk):
        kb = k_scr[n * ATTN_BLOCK:(n + 2) * ATTN_BLOCK, :]
        vb = v_scr[n * ATTN_BLOCK:(n + 2) * ATTN_BLOCK, :]
        outs = []
        for g in range(ATTN_KV_HEADS):
            lhs = jnp.concatenate(
                [q_scr[g, c * ts + n * ATTN_BLOCK:c * ts + (n + 1) * ATTN_BLOCK, :] for c in range(4)],
                axis=0)
            s_all = _mm_nt(lhs, kb)
            probs, scales = [], []
            for r in range(4 * ATTN_BLOCK // SOFTMAX_ROWS):
                r0 = r * SOFTMAX_ROWS
                q0 = r0 % ATTN_BLOCK
                sink = sinks_ref[layer, 4 * g + r0 // ATTN_BLOCK]
                s_prev = s_all[r0:r0 + SOFTMAX_ROWS, :ATTN_BLOCK]
                if n == 0:
                    s_prev = s_prev + first_block_bias
                prev_sel = from_prev[q0:q0 + SOFTMAX_ROWS]
                s = jnp.where(prev_sel, s_prev, s_all[r0:r0 + SOFTMAX_ROWS, ATTN_BLOCK:])
                m = jnp.maximum(jnp.max(s, axis=1, keepdims=True), sink)
                p = jnp.exp(s - m)
                denom = jnp.sum(p, axis=1, keepdims=True) + jnp.exp(sink - m)
                pb = p.astype(BF16)
                zero = jnp.zeros_like(pb)
                probs.append(jnp.concatenate(
                    [jnp.where(prev_sel, pb, zero), jnp.where(prev_sel, zero, pb)], axis=1))
                scales.append(1.0 / denom)
            outs.append(_mm(jnp.concatenate(probs, axis=0), vb) * jnp.concatenate(scales, axis=0))
            drain(tasks, (per_blk + 1 - g) // 2)
        for c in range(4):
            bra_scr[n * ATTN_BLOCK:(n + 1) * ATTN_BLOCK, c * LANES:(c + 1) * LANES] = jnp.where(
                low_half,
                outs[0][c * ATTN_BLOCK:(c + 1) * ATTN_BLOCK],
                outs[1][c * ATTN_BLOCK:(c + 1) * ATTN_BLOCK])
    drain(tasks, len(tasks))
    k_scr[0:ATTN_BLOCK, :] = k_scr[ts:ts + ATTN_BLOCK, :]
    v_scr[0:ATTN_BLOCK, :] = v_scr[ts:ts + ATTN_BLOCK, :]
    ya_scr[...] = (bra_scr[...] * sga_scr[...]).astype(BF16)

    psm = psm_scr[...]
    logit = _mm(psm.astype(BF16), wup_ref[...]) + prm(P_BGATE)
    log_alpha = (jnp.minimum(logit, 0.0) - jnp.log(1.0 + jnp.exp(-jnp.abs(logit)))) * (
        1.0 / GLA_GATE_NORMALIZER)
    dt = _softplus(psm + prm(P_DTB))
    adt = dt * (-jnp.exp(prm(P_ALOG)))
    cs_in = jnp.concatenate([log_alpha, adt], axis=1)
    ltri = ltri_ref[...]
    cs_rows = ltri.shape[0]
    cs_parts = []
    for r in range(ts // cs_rows):
        pieces = _split3(cs_in[r * cs_rows:(r + 1) * cs_rows])
        cs_parts.append(_mm(ltri, pieces[0]) + _mm(ltri, pieces[1]) + _mm(ltri, pieces[2]))
    cs = cs_parts[0] if len(cs_parts) == 1 else jnp.concatenate(cs_parts, axis=0)
    bcum = cs[:, :GLA_HEADS * GLA_DK]
    acs = cs[:, GLA_HEADS * GLA_DK:]

    tasks = [merge_tile(ya_scr, 0, gate_ac_scr, t * MXU_TILE, True) for t in range(4)]
    tasks += [proj_tile(W_M + D_MODEL + t * MXU_TILE, gate_b_scr, t * MXU_TILE, jnp.tanh)
              for t in range(4)]
    per_chunk = -(-len(tasks) // nchunk)

    dkw = GLA_HEADS * GLA_DK
    lane_k = lax.broadcasted_iota(jnp.int32, (CHUNK, dkw), 1) // GLA_DK
    ci = lax.broadcasted_iota(jnp.int32, (GLA_HEADS * CHUNK, CHUNK), 0) % CHUNK
    cj = lax.broadcasted_iota(jnp.int32, (GLA_HEADS * CHUNK, CHUNK), 1)
    causal4 = ci >= cj
    st_blockdiag = (lax.broadcasted_iota(jnp.int32, gla_scr.shape, 0) // GLA_DK
                    == lax.broadcasted_iota(jnp.int32, gla_scr.shape, 1) // GLA_DV)
    gla_state = gla_scr[...]
    for c in range(nchunk):
        rs = slice(c * CHUNK, (c + 1) * CHUNK)
        bc = bcum[rs]
        bmid = bc[CHUNK // 2 - 1:CHUNK // 2]
        blast = bc[CHUNK - 1:CHUNK]
        q = gqk_scr[rs, :dkw] * (GLA_DK ** -0.5)
        k = gqk_scr[rs, dkw:]
        v = gv_scr[rs, :]
        q_dec = (q * jnp.exp(bc)).astype(BF16)
        q_mid = q * jnp.exp(bc - bmid)
        k_mid = (k * jnp.exp(bmid - bc)).astype(BF16)
        k_end = (k * jnp.exp(blast - bc)).astype(BF16)
        lhs = jnp.concatenate(
            [jnp.where(lane_k == h, q_mid, 0.0).astype(BF16) for h in range(GLA_HEADS)], axis=0)
        attn = jnp.where(causal4, _mm_nt(lhs, k_mid), 0.0).astype(BF16)
        o_inter = _mm(q_dec, gla_state.astype(BF16))
        drain(tasks, per_chunk // 2)
        for h in range(GLA_HEADS):
            hs = slice(h * GLA_DV, (h + 1) * GLA_DV)
            o = _mm(attn[h * CHUNK:(h + 1) * CHUNK], v[:, hs]) + o_inter[:, hs]
            brb_scr[rs, hs] = o * lax.rsqrt(jnp.mean(o * o, axis=-1, keepdims=True) + EPS)
        decay_col = jnp.broadcast_to(jnp.exp(blast), (LANES, dkw)).T
        decay_col = jnp.concatenate([decay_col] * (gla_scr.shape[1] // LANES), axis=1)
        gla_state = jnp.where(st_blockdiag, gla_state * decay_col + _mm_tn(k_end, v), 0.0)
        drain(tasks, per_chunk - per_chunk // 2)
    drain(tasks, len(tasks))
    gla_scr[...] = gla_state
    yb_scr[...] = (brb_scr[...] * prm(P_GNORM) * sgg_scr[...]).astype(BF16)

    tasks = [merge_tile(yb_scr, 1, gate_b_scr, t * MXU_TILE, False) for t in range(4)]
    tasks += [proj_tile(W_M + 2 * D_MODEL + t * MXU_TILE, gate_ac_scr, t * MXU_TILE, jnp.tanh)
              for t in range(4)]
    per_chunk = -(-len(tasks) // nchunk)

    emat = emat_ref[...]
    tril = (lax.broadcasted_iota(jnp.int32, (CHUNK, CHUNK), 0)
            >= lax.broadcasted_iota(jnp.int32, (CHUNK, CHUNK), 1))
    gw = SSD_D_INNER // SSD_GROUPS
    ssd_state = ssd_scr[...]
    a_last_rows = jnp.concatenate(
        [jnp.broadcast_to(acs[(c + 1) * CHUNK - 1:(c + 1) * CHUNK], (CHUNK, LANES)) for c in range(nchunk)], axis=0)
    w_all = dt * jnp.exp(a_last_rows - acs)
    w_hi = w_all.astype(BF16)
    w_lo = (w_all - w_hi.astype(F32)).astype(BF16)
    w_exp = _mm(w_hi, emat) + _mm(w_lo, emat)
    al3 = _split3(jnp.concatenate(
        [jnp.broadcast_to(acs[(c + 1) * CHUNK - 1:(c + 1) * CHUNK], (8, LANES)) for c in range(nchunk)], axis=0))
    chunk_decays = jnp.exp(_mm(al3[0], emat) + _mm(al3[1], emat) + _mm(al3[2], emat))
    nslab = conv_scr.shape[0]
    row = lambda r, sl: jnp.broadcast_to(prm_ref[r:r + 1, sl * LANES:(sl + 1) * LANES] * 0.5, (8, LANES))
    half_convb = [row(P_CONVB[0], sl) for sl in range(nslab)]
    half_convw = [[row(P_CONVW[0] + i, sl) for i in range(SSD_CONV)] for sl in range(nslab)]
    for c in range(nchunk):
        rs = slice(c * CHUNK, (c + 1) * CHUNK)
        for sl in range(nslab):
            for grp in range(CHUNK // CONV_GROUP):
                base = CONV_PAD + c * CHUNK + grp * CONV_GROUP
                taps = [conv_scr[sl, pl.ds(base + m, 8, stride=CONV_STRIDE), :]
                        for m in range(1 - SSD_CONV, CONV_STRIDE)]
                for m in range(CONV_STRIDE):
                    acc = half_convb[sl]
                    for i in range(SSD_CONV):
                        acc = acc + half_convw[sl][i] * taps[m + i]
                    xbc_scr[sl, pl.ds(c * CHUNK + grp * CONV_GROUP + m, 8, stride=CONV_STRIDE), :] = (
                        _silu_of_double(acc))
        nx = SSD_D_INNER // LANES
        xs = jnp.concatenate([xbc_scr[sl, rs, :] for sl in range(nx)], axis=1)
        bmat = [xbc_scr[nx + g, rs, :].astype(BF16) for g in range(SSD_GROUPS)]
        cmat = [xbc_scr[nx + SSD_GROUPS + g, rs, :] for g in range(SSD_GROUPS)]
        a = acs[rs]
        xw = (xs * w_exp[rs]).astype(BF16)
        chunk_decay = chunk_decays[8 * c:8 * c + 1]
        a_t = a.T
        dt_t = dt[rs].T
        x_c = xs.astype(BF16)
        state_b = ssd_state.astype(BF16)
        drain(tasks, per_chunk // 2)
        ys, new_state = [], []
        for g in range(SSD_GROUPS):
            b_g = bmat[g]
            c_g = cmat[g]
            cb = _mm_nt(c_g.astype(BF16), b_g)
            for hh in range(SSD_HEADS // SSD_GROUPS):
                h = g * (SSD_HEADS // SSD_GROUPS) + hh
                a_col = a[:, h:h + 1]
                decay = jnp.where(tril, jnp.exp(a_col - a_t[h:h + 1, :]), 0.0)
                m_h = cb * decay * dt_t[h:h + 1, :]
                lhs = jnp.concatenate([m_h, c_g * jnp.exp(a_col)], axis=1).astype(BF16)
                ys.append(lhs)
            ws = slice(g * gw, (g + 1) * gw)
            new_state.append(ssd_state[:, ws] * chunk_decay[:, ws] + _mm_tn(b_g, xw[:, ws]))
        ssd_state = jnp.concatenate(new_state, axis=1)
        pair_out = []
        for p in range(SSD_HEADS // 2):
            ps = slice(p * LANES, (p + 1) * LANES)
            rhs = jnp.concatenate([x_c[:, ps], state_b[:, ps]], axis=0)
            both = _mm(jnp.concatenate([ys[2 * p], ys[2 * p + 1]], axis=0), rhs)
            pair_out.append(jnp.where(low_half, both[:CHUNK], both[CHUNK:]))
        y = jnp.concatenate(pair_out, axis=1)
        u = (y + xs * prm(P_DEXP)) * sz_scr[rs, :]
        normed = []
        for g in range(SSD_GROUPS):
            ug = u[:, g * gw:(g + 1) * gw]
            normed.append(ug * lax.rsqrt(jnp.mean(ug * ug, axis=-1, keepdims=True) + EPS))
        brc_scr[rs, :] = (jnp.concatenate(normed, axis=1) * prm(P_SNORM)).astype(BF16)
        drain(tasks, per_chunk - per_chunk // 2)
    drain(tasks, len(tasks))
    ssd_scr[...] = ssd_state
    conv_scr[:, 0:CONV_PAD, :] = conv_scr[:, ts:ts + CONV_PAD, :]

    u_c = _mm(brc_scr[...], wbr_ref[2])
    merged = merged_scr[...] + (gate_ac_scr[...] * u_c + u_c)
    out_ref[0] = x_ref[0] + _mm(merged.astype(BF16), wo_ref[...])


def _to_ya_lanes(w, axis):
    shape = w.shape
    w = w.reshape(*shape[:axis], ATTN_KV_HEADS, 4, ATTN_HEAD_DIM, *shape[axis + 1:])
    return jnp.swapaxes(w, axis, axis + 1).reshape(shape)


def _const_tables():
    lane = np.arange(LANES)
    gmat = ((lane[:, None] // 32) % 2 == (lane[None, :] // 32) % 2).astype(np.float32) / ATTN_HEAD_DIM
    r = np.arange(2 * CHUNK)
    ltri = ((r[:, None] // CHUNK == r[None, :] // CHUNK) & (r[:, None] >= r[None, :])).astype(np.float32)
    emat = np.zeros((LANES, SSD_D_INNER), np.float32)
    for h in range(SSD_HEADS):
        emat[h, h * SSD_HEAD_DIM:(h + 1) * SSD_HEAD_DIM] = 1.0
    return (jnp.asarray(gmat, BF16), jnp.asarray(ltri, BF16), jnp.asarray(emat, BF16))


def _weight_prep_kernel(wt_ref, o_ref):
    c = IN_PROJ_CUTS

    def put(dst, srcs, width, scale=None):
        pieces = [wt_ref[s:s + width, :] for s in srcs]
        v = pieces[0] if len(pieces) == 1 else jnp.concatenate(pieces, axis=0)
        for k in range(v.shape[0] // LANES):
            t = v[k * LANES:(k + 1) * LANES, :].T
            o_ref[:, dst + k * LANES:dst + (k + 1) * LANES] = (t if scale is None else t * scale).astype(BF16)

    for blk in range(4):
        put(W_Q + blk * LANES, [c[0] + (blk + 4 * (r % 2)) * 64 + 32 * (r // 2) for r in range(4)], 32)
        put(W_GA + blk * LANES, [c[3] + (blk + 4 * g) * 64 for g in range(2)], 64, 0.5)
    put(W_KV, [c[1] + (r % 2) * 64 + 32 * (r // 2) for r in range(4)], 32)
    put(W_KV + LANES, [c[2]], LANES)
    put(W_GQK, [c[4]], 512)
    put(W_GV, [c[6]], 512)
    put(W_GG, [c[7]], 512, 0.5)
    small = jnp.concatenate([wt_ref[c[10]:c[11], :], wt_ref[c[8]:c[9], :],
                             jnp.zeros((LANES - 24, wt_ref.shape[1]), F32)], axis=0)
    o_ref[:, W_SM:W_SM + LANES] = small.T.astype(BF16)
    put(W_XBC, [c[9]], 1024)
    put(W_Z, [c[11]], 512, 0.5)
    put(W_M, [c[12]], 3 * D_MODEL, 0.5)


def _prep_in_proj(w_in):
    depth, d_in, n = w_in.shape
    return pl.pallas_call(
        _weight_prep_kernel,
        grid=(depth, d_in // PREP_ROWS),
        in_specs=[pl.BlockSpec((None, n, PREP_ROWS), lambda l, r: (l, 0, r))],
        out_specs=pl.BlockSpec((None, PREP_ROWS, W_END), lambda l, r: (l, r, 0)),
        out_shape=jax.ShapeDtypeStruct((depth, d_in, W_END), BF16),
        compiler_params=pltpu.CompilerParams(vmem_limit_bytes=VMEM_LIMIT_BYTES),
        name="weight_prep",
    )(jnp.swapaxes(w_in, 1, 2))


def _prep_params(norm_g, w_in, attn_q_norm, attn_k_norm, gla_w_gate_up, gla_b_gate, gla_out_norm,
                 ssd_conv_w, ssd_conv_b, ssd_dt_bias, ssd_A_log, ssd_D, ssd_out_norm, w_branch, w_out):
    depth = w_in.shape[0]
    w_cat = _prep_in_proj(w_in)
    wbr = (0.5 * jnp.concatenate([_to_ya_lanes(w_branch[:, 0:1], 2), w_branch[:, 1:]], axis=1)).astype(BF16)
    wup = jnp.zeros((depth, LANES, gla_w_gate_up.shape[-1]), F32).at[:, 8:8 + GLA_GATE_RANK].set(
        gla_w_gate_up).astype(BF16)
    lane_gain = lambda g: jnp.repeat(g.reshape(depth, 2, 1, 32), 2, axis=2).reshape(depth, LANES)
    pad = lambda v, n: jnp.pad(v.astype(F32), ((0, 0), (0, n - v.shape[-1])))
    prm = jnp.zeros((depth, P_ROWS, D_MODEL), F32)
    prm = prm.at[:, P_NORM[0]].set(norm_g)
    prm = prm.at[:, P_CONVB[0]].set(ssd_conv_b)
    prm = prm.at[:, P_CONVW[0]:P_CONVW[0] + SSD_CONV].set(ssd_conv_w)
    prm = prm.at[:, P_QGAIN[0]].set(jnp.concatenate([
        lane_gain(attn_q_norm), lane_gain(attn_k_norm), pad(ssd_dt_bias, LANES), pad(ssd_A_log, LANES),
        gla_b_gate, jnp.zeros((depth, D_MODEL - 768), F32)], axis=-1))
    prm = prm.at[:, P_GNORM[0]].set(jnp.concatenate([
        jnp.tile(gla_out_norm, (1, GLA_HEADS)), jnp.repeat(ssd_D, SSD_HEAD_DIM, axis=-1)], axis=-1))
    prm = prm.at[:, P_SNORM[0], :SSD_D_INNER].set(ssd_out_norm)
    return w_cat, wbr, w_out.astype(BF16), wup, prm


def _layer_call(x, cos_t, sin_t, sinks, params, consts, layer, ts):
    bsz, seq, d = x.shape
    tok = lambda w: pl.BlockSpec((1, ts, w), lambda b, j: (b, j, 0))

    def layer_spec(arr):
        nd = arr.ndim - 1
        return pl.BlockSpec((None,) + arr.shape[1:], lambda b, j, _nd=nd: (layer,) + (0,) * _nd,
                            pipeline_mode=pl.Buffered(1))

    def const_spec(arr):
        return pl.BlockSpec(arr.shape, lambda b, j, _nd=arr.ndim: (0,) * _nd, pipeline_mode=pl.Buffered(1))

    rope_spec = pl.BlockSpec((1, ts // ROPE_TOKENS, LANES), lambda b, j: (b, j, 0))
    in_specs = [pl.BlockSpec(memory_space=pltpu.SMEM), tok(d), rope_spec, rope_spec]
    in_specs += [layer_spec(a) for a in params] + [const_spec(a) for a in consts]
    conv_slabs = (SSD_D_INNER + 2 * SSD_GROUPS * SSD_D_STATE) // LANES
    scratch = [
        pltpu.VMEM((ts, d), BF16),
        pltpu.VMEM((ATTN_KV_HEADS, 4 * ts, LANES), BF16),
        pltpu.VMEM((ts + ATTN_BLOCK, LANES), BF16),
        pltpu.VMEM((ts + ATTN_BLOCK, LANES), BF16),
        pltpu.VMEM((ts, 512), F32),
        pltpu.VMEM((ts, 512), F32),
        pltpu.VMEM((ts, 512), BF16),
        pltpu.VMEM((ts, d), F32),
        pltpu.VMEM((conv_slabs, ts + CONV_PAD, LANES), F32),
        pltpu.VMEM((conv_slabs, ts, LANES), F32),
        pltpu.VMEM((GLA_HEADS * GLA_DK, GLA_HEADS * GLA_DV), F32),
        pltpu.VMEM((SSD_D_STATE, SSD_D_INNER), F32),
        pltpu.VMEM((ts, 512), F32),
        pltpu.VMEM((ts, 512), BF16),
        pltpu.VMEM((ts, LANES), F32),
        pltpu.VMEM((ts, 512), F32),
        pltpu.VMEM((ts, 512), F32),
        pltpu.VMEM((ts, 512), F32),
        pltpu.VMEM((ts, d), F32),
        pltpu.VMEM((ts, d), F32),
        pltpu.VMEM((ts, 512), BF16),
        pltpu.VMEM((ts, 512), BF16),
        pltpu.VMEM((ts, LANES), F32),
        pltpu.VMEM((ts, LANES), F32),
    ]
    return pl.pallas_call(
        functools.partial(_layer_kernel, ts=ts, layer=layer),
        grid=(bsz, seq // ts),
        in_specs=in_specs,
        out_specs=pl.BlockSpec((1, ts, d), lambda b, j: (b, j, 0)),
        out_shape=jax.ShapeDtypeStruct(x.shape, x.dtype),
        scratch_shapes=scratch,
        compiler_params=pltpu.CompilerParams(
            dimension_semantics=("arbitrary", "arbitrary"),
            vmem_limit_bytes=VMEM_LIMIT_BYTES),
        name="hybrid_layer",
    )(sinks.astype(F32), x, cos_t, sin_t, *params, *consts)


def _rope_tables(positions):
    bsz, seq = positions.shape
    nfreq = ATTN_HEAD_DIM // 2
    per_row = LANES // nfreq
    inv_freq = ROPE_THETA ** (-jnp.arange(0, ATTN_HEAD_DIM, 2, dtype=F32) / ATTN_HEAD_DIM)
    freq = jnp.tile(inv_freq, per_row).reshape(1, LANES)
    pos = jnp.repeat(positions.reshape(bsz, seq // per_row, per_row), nfreq, axis=2)
    spec = pl.BlockSpec((1, seq // per_row, LANES), lambda b: (b, 0, 0))
    return pl.pallas_call(
        _rope_table_kernel,
        grid=(bsz,),
        in_specs=[spec, pl.BlockSpec((1, LANES), lambda b: (0, 0))],
        out_specs=[spec, spec],
        out_shape=[jax.ShapeDtypeStruct((bsz, seq // per_row, LANES), F32)] * 2,
        name="rope_tables",
    )(pos, freq)


def kernel(x, positions, norm_g, w_in, attn_q_norm, attn_k_norm, attn_sinks, gla_w_gate_up, gla_b_gate,
           gla_out_norm, ssd_conv_w, ssd_conv_b, ssd_dt_bias, ssd_A_log, ssd_D, ssd_out_norm,
           w_branch, w_out):
    seq = x.shape[1]
    ts = min(SEQ_BLOCK, seq)
    assert seq % ts == 0 and ts % (2 * CHUNK) == 0 and x.shape[2] == D_MODEL
    cos_t, sin_t = _rope_tables(positions)
    consts = _const_tables()
    params = _prep_params(norm_g, w_in, attn_q_norm, attn_k_norm, gla_w_gate_up, gla_b_gate, gla_out_norm,
                          ssd_conv_w, ssd_conv_b, ssd_dt_bias, ssd_A_log, ssd_D, ssd_out_norm,
                          w_branch, w_out)
    for layer in range(w_in.shape[0]):
        x = _layer_call(x, cos_t, sin_t, attn_sinks, params, consts, layer, ts)
    return x
```

```python
import functools

import numpy as np
import jax
import jax.numpy as jnp
from jax import lax
from jax.experimental import pallas as pl
from jax.experimental.pallas import tpu as pltpu

F32 = jnp.float32
BF16 = jnp.bfloat16

D_MODEL = 1024
EPS = 1e-6
ROPE_THETA = 10000.0

ATTN_HEADS = 8
ATTN_KV_HEADS = 2
ATTN_HEAD_DIM = 64
ATTN_BLOCK = 128
GLA_HEADS = 4
GLA_DK = 64
GLA_DV = 128
GLA_GATE_RANK = 16
GLA_GATE_NORMALIZER = 16.0
SSD_HEADS = 8
SSD_HEAD_DIM = 64
SSD_GROUPS = 2
SSD_D_STATE = 128
SSD_D_INNER = 512
SSD_CONV = 4

IN_PROJ_SIZES = (512, 128, 128, 512, 256, 256, 512, 512, 16, 1024, 8, 512, 3072)
IN_PROJ_CUTS = tuple(int(v) for v in np.cumsum((0,) + IN_PROJ_SIZES))

LANES = 128
CHUNK = 128
CONV_PAD = 8
SEQ_BLOCK = 512
PREP_ROWS = 256
VMEM_LIMIT_BYTES = 56 * 1024 * 1024
MASK_VALUE = -1e30
MXU_TILE = 256
SOFTMAX_ROWS = 64
ROPE_TOKENS = 4
CONV_STRIDE = 4
CONV_GROUP = 8 * CONV_STRIDE

W_Q, W_KV, W_GA, W_GQK, W_GV, W_GG, W_SM, W_XBC, W_Z, W_M, W_END = (
    0, 512, 768, 1280, 1792, 2304, 2816, 2944, 3968, 4480, 7552)
P_NORM, P_CONVB, P_CONVW = (0, 0, 1024), (1, 0, 1024), (2, 0, 1024)
P_QGAIN, P_KGAIN, P_DTB, P_ALOG, P_BGATE = (6, 0, 128), (6, 128, 128), (6, 256, 128), (6, 384, 128), (6, 512, 256)
P_GNORM, P_DEXP = (7, 0, 512), (7, 512, 512)
P_SNORM = (8, 0, 512)
P_ROWS = 16


def _mm(a, b):
    return jnp.dot(a, b, preferred_element_type=F32)


def _mm_nt(a, b):
    return lax.dot_general(a, b, (((1,), (1,)), ((), ())), preferred_element_type=F32)


def _mm_tn(a, b):
    return lax.dot_general(a, b, (((0,), (0,)), ((), ())), preferred_element_type=F32)


def _silu_of_double(vh):
    return vh * jnp.tanh(vh) + vh


def _softplus(v):
    return jnp.maximum(v, 0.0) + jnp.log(1.0 + jnp.exp(-jnp.abs(v)))


def _split3(v):
    hi = v.astype(BF16)
    r1 = v - hi.astype(F32)
    mid = r1.astype(BF16)
    lo = (r1 - mid.astype(F32)).astype(BF16)
    return hi, mid, lo


def _rope_table_kernel(pos_ref, freq_ref, cos_ref, sin_ref):
    ang = pos_ref[0].astype(F32) * freq_ref[...]
    cos_ref[0] = jnp.cos(ang)
    sin_ref[0] = jnp.sin(ang)


def _layer_kernel(
        sinks_ref, x_ref, cos_ref, sin_ref, w_ref, wbr_ref, wo_ref, wup_ref, prm_ref,
        gmat_ref, ltri_ref, emat_ref,
        out_ref,
        hb_scr, q_scr, k_scr, v_scr, bra_scr, brb_scr, brc_scr, merged_scr, conv_scr, xbc_scr,
        gla_scr, ssd_scr, gqk_scr, gv_scr, psm_scr, sga_scr, sgg_scr, sz_scr, gate_ac_scr, gate_b_scr,
        ya_scr, yb_scr, cos_scr, sin_scr,
        *, ts, layer):
    nblk = ts // ATTN_BLOCK
    nchunk = ts // CHUNK
    j = pl.program_id(1)

    @pl.when(j == 0)
    def _reset_carries():
        k_scr[0:ATTN_BLOCK, :] = jnp.zeros((ATTN_BLOCK, LANES), BF16)
        v_scr[0:ATTN_BLOCK, :] = jnp.zeros((ATTN_BLOCK, LANES), BF16)
        conv_scr[:, 0:CONV_PAD, :] = jnp.zeros((conv_scr.shape[0], CONV_PAD, LANES), F32)
        gla_scr[...] = jnp.zeros(gla_scr.shape, F32)
        ssd_scr[...] = jnp.zeros(ssd_scr.shape, F32)

    def prm(p):
        return prm_ref[p[0]:p[0] + 1, p[1]:p[1] + p[2]]

    def proj_tile(c0, dst_ref, d0, fn=None, width=MXU_TILE):
        def run():
            t = _mm(hb_scr[...], w_ref[:, c0:c0 + width])
            dst_ref[:, d0:d0 + width] = (t if fn is None else fn(t)).astype(dst_ref.dtype)
        return run

    def conv_in_tile(c0):
        def run():
            t = _mm(hb_scr[...], w_ref[:, W_XBC + c0:W_XBC + c0 + MXU_TILE])
            for i in range(MXU_TILE // LANES):
                conv_scr[c0 // LANES + i, CONV_PAD:CONV_PAD + ts, :] = t[:, i * LANES:(i + 1) * LANES]
        return run

    def merge_tile(y_ref, branch, gate_ref, c0, first):
        def run():
            u = _mm(y_ref[...], wbr_ref[branch, :, c0:c0 + MXU_TILE])
            g = gate_ref[:, c0:c0 + MXU_TILE] * u + u
            if first:
                merged_scr[:, c0:c0 + MXU_TILE] = g
            else:
                merged_scr[:, c0:c0 + MXU_TILE] += g
        return run

    def drain(tasks, count):
        for _ in range(min(count, len(tasks))):
            tasks.pop(0)()

    half = ts // 4
    for r in range(4):
        x = x_ref[0, r * half:(r + 1) * half, :]
        ms = jnp.mean(x * x, axis=-1, keepdims=True)
        hb_scr[r * half:(r + 1) * half, :] = (x * lax.rsqrt(ms + EPS) * prm(P_NORM)).astype(BF16)

    lane = lax.broadcasted_iota(jnp.int32, (ATTN_BLOCK, LANES), 1)
    low_half = lane < LANES // 2

    lane_q = lax.broadcasted_iota(jnp.int32, (ts // ROPE_TOKENS, LANES), 1) // (LANES // ROPE_TOKENS)
    for src_ref, dst_scr in ((cos_ref, cos_scr), (sin_ref, sin_scr)):
        compact = src_ref[0]
        for p in range(ROPE_TOKENS):
            one = jnp.where(lane_q == p, compact, 0.0)
            two = one + pltpu.roll(one, LANES // 4, 1)
            dst_scr[pl.ds(p, ts // ROPE_TOKENS, stride=ROPE_TOKENS), :] = two + pltpu.roll(two, LANES // 2, 1)
    lane_ts = lax.broadcasted_iota(jnp.int32, (ts, LANES), 1)
    cos_t = cos_scr[...]
    sin_t = jnp.where(lane_ts < LANES // 2, -sin_scr[...], sin_scr[...])
    gmat = gmat_ref[...]
    akv = jnp.concatenate([_mm(hb_scr[r * half:(r + 1) * half, :], w_ref[:, W_KV:W_GA]) for r in range(4)],
                          axis=0)
    ak = akv[:, :LANES]
    kn = ak * lax.rsqrt(_mm((ak * ak).astype(BF16), gmat) + EPS) * prm(P_KGAIN)
    kr = kn * cos_t + pltpu.roll(kn, LANES // 2, 1) * sin_t
    k_scr[ATTN_BLOCK:ATTN_BLOCK + ts, :] = kr.astype(BF16)
    v_scr[ATTN_BLOCK:ATTN_BLOCK + ts, :] = akv[:, LANES:].astype(BF16)

    aq = jnp.concatenate([_mm(hb_scr[r * half:(r + 1) * half, :], w_ref[:, W_Q:W_KV]) for r in range(4)],
                         axis=0)
    group0 = (lane_ts // 32) % 2 == 0
    qgain = prm(P_QGAIN) * (ATTN_HEAD_DIM ** -0.5)
    for c in range(4):
        qc = aq[:, c * LANES:(c + 1) * LANES]
        qn = qc * lax.rsqrt(_mm((qc * qc).astype(BF16), gmat) + EPS) * qgain
        qr = qn * cos_t + pltpu.roll(qn, LANES // 2, 1) * sin_t
        q_scr[0, c * ts:(c + 1) * ts, :] = jnp.where(group0, qr, 0.0).astype(BF16)
        q_scr[1, c * ts:(c + 1) * ts, :] = jnp.where(group0, 0.0, qr).astype(BF16)

    tasks = [proj_tile(W_SM, psm_scr, 0, width=LANES)]
    tasks += [proj_tile(W_GQK + t * MXU_TILE, gqk_scr, t * MXU_TILE) for t in range(2)]
    tasks += [proj_tile(W_GV + t * MXU_TILE, gv_scr, t * MXU_TILE) for t in range(2)]
    tasks += [conv_in_tile(t * MXU_TILE) for t in range(4)]
    drain(tasks, len(tasks))
    tasks = [proj_tile(W_GA + t * MXU_TILE, sga_scr, t * MXU_TILE, _silu_of_double) for t in range(2)]
    tasks += [proj_tile(W_GG + t * MXU_TILE, sgg_scr, t * MXU_TILE, _silu_of_double) for t in range(2)]
    tasks += [proj_tile(W_Z + t * MXU_TILE, sz_scr, t * MXU_TILE, _silu_of_double) for t in range(2)]
    tasks += [proj_tile(W_M + t * MXU_TILE, gate_ac_scr, t * MXU_TILE, jnp.tanh) for t in range(4)]
    per_blk = -(-len(tasks) // nblk)

    qi = lax.broadcasted_iota(jnp.int32, (ATTN_BLOCK, ATTN_BLOCK), 0)
    ki = lax.broadcasted_iota(jnp.int32, (ATTN_BLOCK, ATTN_BLOCK), 1)
    from_prev = ki > qi
    first_block_bias = jnp.where(j > 0, 0.0, MASK_VALUE)
    for n in range(nblk):
        kb = k_scr[n * ATTN_BLOCK:(n + 2) * ATTN_BLOCK, :]
        vb = v_scr[n * ATTN_BLOCK:(n + 2) * ATTN_BLOCK, :]
        outs = []
        for g in range(ATTN_KV_HEADS):
            lhs = jnp.concatenate(
                [q_scr[g, c * ts + n * ATTN_BLOCK:c * ts + (n + 1) * ATTN_BLOCK, :] for c in range(4)],
                axis=0)
            s_all = _mm_nt(lhs, kb)
            probs, scales = [], []
            for r in range(4 * ATTN_BLOCK // SOFTMAX_ROWS):
                r0 = r * SOFTMAX_ROWS
                q0 = r0 % ATTN_BLOCK
                sink = sinks_ref[layer, 4 * g + r0 // ATTN_BLOCK]
                s_prev = s_all[r0:r0 + SOFTMAX_ROWS, :ATTN_BLOCK]
                if n == 0:
                    s_prev = s_prev + first_block_bias
                prev_sel = from_prev[q0:q0 + SOFTMAX_ROWS]
                s = jnp.where(prev_sel, s_prev, s_all[r0:r0 + SOFTMAX_ROWS, ATTN_BLOCK:])
                m = jnp.maximum(jnp.max(s, axis=1, keepdims=True), sink)
                p = jnp.exp(s - m)
                denom = jnp.sum(p, axis=1, keepdims=True) + jnp.exp(sink - m)
                pb = p.astype(BF16)
                zero = jnp.zeros_like(pb)
                probs.append(jnp.concatenate(
                    [jnp.where(prev_sel, pb, zero), jnp.where(prev_sel, zero, pb)], axis=1))
                scales.append(1.0 / denom)
            outs.append(_mm(jnp.concatenate(probs, axis=0), vb) * jnp.concatenate(scales, axis=0))
            drain(tasks, (per_blk + 1 - g) // 2)
        for c in range(4):
            bra_scr[n * ATTN_BLOCK:(n + 1) * ATTN_BLOCK, c * LANES:(c + 1) * LANES] = jnp.where(
                low_half,
                outs[0][c * ATTN_BLOCK:(c + 1) * ATTN_BLOCK],
                outs[1][c * ATTN_BLOCK:(c + 1) * ATTN_BLOCK])
    drain(tasks, len(tasks))
    k_scr[0:ATTN_BLOCK, :] = k_scr[ts:ts + ATTN_BLOCK, :]
    v_scr[0:ATTN_BLOCK, :] = v_scr[ts:ts + ATTN_BLOCK, :]
    ya_scr[...] = (bra_scr[...] * sga_scr[...]).astype(BF16)

    psm = psm_scr[...]
    logit = _mm(psm.astype(BF16), wup_ref[...]) + prm(P_BGATE)
    log_alpha = (jnp.minimum(logit, 0.0) - jnp.log(1.0 + jnp.exp(-jnp.abs(logit)))) * (
        1.0 / GLA_GATE_NORMALIZER)
    dt = _softplus(psm + prm(P_DTB))
    adt = dt * (-jnp.exp(prm(P_ALOG)))
    ltri = ltri_ref[...]
    cs_rows = ltri.shape[0]
    b_parts, a_parts = [], []
    for r in range(ts // cs_rows):
        rows = slice(r * cs_rows, (r + 1) * cs_rows)
        la_hi = log_alpha[rows].astype(BF16)
        la_lo = (log_alpha[rows] - la_hi.astype(F32)).astype(BF16)
        b_parts.append(_mm(ltri, la_hi) + _mm(ltri, la_lo))
        pieces = _split3(adt[rows])
        a_parts.append(_mm(ltri, pieces[0]) + _mm(ltri, pieces[1]) + _mm(ltri, pieces[2]))
    bcum = jnp.concatenate(b_parts, axis=0)
    acs = jnp.concatenate(a_parts, axis=0)

    tasks = [merge_tile(ya_scr, 0, gate_ac_scr, t * MXU_TILE, True) for t in range(4)]
    tasks += [proj_tile(W_M + D_MODEL + t * MXU_TILE, gate_b_scr, t * MXU_TILE, jnp.tanh)
              for t in range(4)]
    per_chunk = -(-len(tasks) // nchunk)

    dkw = GLA_HEADS * GLA_DK
    lane_k = lax.broadcasted_iota(jnp.int32, (CHUNK, dkw), 1) // GLA_DK
    ci = lax.broadcasted_iota(jnp.int32, (GLA_HEADS * CHUNK, CHUNK), 0) % CHUNK
    cj = lax.broadcasted_iota(jnp.int32, (GLA_HEADS * CHUNK, CHUNK), 1)
    causal4 = ci >= cj
    st_blockdiag = (lax.broadcasted_iota(jnp.int32, gla_scr.shape, 0) // GLA_DV
                    == lax.broadcasted_iota(jnp.int32, gla_scr.shape, 1) // GLA_DK)
    gla_state = gla_scr[...]
    for c in range(nchunk):
        rs = slice(c * CHUNK, (c + 1) * CHUNK)
        bc = bcum[rs]
        bmid = bc[CHUNK // 2 - 1:CHUNK // 2]
        blast = bc[CHUNK - 1:CHUNK]
        q = gqk_scr[rs, :dkw] * (GLA_DK ** -0.5)
        k = gqk_scr[rs, dkw:]
        v = gv_scr[rs, :]
        q_dec = (q * jnp.exp(bc)).astype(BF16)
        q_mid = q * jnp.exp(bc - bmid)
        k_mid = (k * jnp.exp(bmid - bc)).astype(BF16)
        k_end = (k * jnp.exp(blast - bc)).astype(BF16)
        attn = jnp.concatenate(
            [jnp.where(causal4[:CHUNK], _mm_nt(jnp.where(lane_k == h, q_mid, 0.0).astype(BF16), k_mid), 0.0)
             for h in range(GLA_HEADS)], axis=0).astype(BF16)
        o_inter = _mm_nt(q_dec, gla_state.astype(BF16))
        drain(tasks, per_chunk // 2)
        for h in range(GLA_HEADS):
            hs = slice(h * GLA_DV, (h + 1) * GLA_DV)
            o = _mm(attn[h * CHUNK:(h + 1) * CHUNK], v[:, hs]) + o_inter[:, hs]
            brb_scr[rs, hs] = o * lax.rsqrt(jnp.mean(o * o, axis=-1, keepdims=True) + EPS)
        gla_state = jnp.where(st_blockdiag, gla_state * jnp.exp(blast) + _mm_tn(v, k_end), 0.0)
        drain(tasks, per_chunk - per_chunk // 2)
    drain(tasks, len(tasks))
    gla_scr[...] = gla_state
    yb_scr[...] = (brb_scr[...] * prm(P_GNORM) * sgg_scr[...]).astype(BF16)

    tasks = [merge_tile(yb_scr, 1, gate_b_scr, t * MXU_TILE, False) for t in range(4)]
    tasks += [proj_tile(W_M + 2 * D_MODEL + t * MXU_TILE, gate_ac_scr, t * MXU_TILE, jnp.tanh)
              for t in range(4)]
    per_chunk = -(-len(tasks) // nchunk)

    emat = emat_ref[...]
    tril = (lax.broadcasted_iota(jnp.int32, (CHUNK, CHUNK), 0)
            >= lax.broadcasted_iota(jnp.int32, (CHUNK, CHUNK), 1))
    gw = SSD_D_INNER // SSD_GROUPS
    ssd_state = ssd_scr[...]
    a_last_rows = jnp.concatenate(
        [jnp.broadcast_to(acs[(c + 1) * CHUNK - 1:(c + 1) * CHUNK], (CHUNK, LANES)) for c in range(nchunk)], axis=0)
    w_all = dt * jnp.exp(a_last_rows - acs)
    w_hi = w_all.astype(BF16)
    w_lo = (w_all - w_hi.astype(F32)).astype(BF16)
    w_exp = _mm(w_hi, emat) + _mm(w_lo, emat)
    al3 = _split3(jnp.concatenate(
        [jnp.broadcast_to(acs[(c + 1) * CHUNK - 1:(c + 1) * CHUNK], (8, LANES)) for c in range(nchunk)], axis=0))
    chunk_decays = jnp.exp(_mm(al3[0], emat) + _mm(al3[1], emat) + _mm(al3[2], emat))
    nslab = conv_scr.shape[0]
    row = lambda r, sl: jnp.broadcast_to(prm_ref[r:r + 1, sl * LANES:(sl + 1) * LANES] * 0.5, (8, LANES))
    half_convb = [row(P_CONVB[0], sl) for sl in range(nslab)]
    half_convw = [[row(P_CONVW[0] + i, sl) for i in range(SSD_CONV)] for sl in range(nslab)]
    for c in range(nchunk):
        rs = slice(c * CHUNK, (c + 1) * CHUNK)
        for sl in range(nslab):
            for grp in range(CHUNK // CONV_GROUP):
                base = CONV_PAD + c * CHUNK + grp * CONV_GROUP
                taps = [conv_scr[sl, pl.ds(base + m, 8, stride=CONV_STRIDE), :]
                        for m in range(1 - SSD_CONV, CONV_STRIDE)]
                for m in range(CONV_STRIDE):
                    acc = half_convb[sl]
                    for i in range(SSD_CONV):
                        acc = acc + half_convw[sl][i] * taps[m + i]
                    xbc_scr[sl, pl.ds(c * CHUNK + grp * CONV_GROUP + m, 8, stride=CONV_STRIDE), :] = (
                        _silu_of_double(acc))
        nx = SSD_D_INNER // LANES
        xs = jnp.concatenate([xbc_scr[sl, rs, :] for sl in range(nx)], axis=1)
        bmat = [xbc_scr[nx + g, rs, :].astype(BF16) for g in range(SSD_GROUPS)]
        cmat = [xbc_scr[nx + SSD_GROUPS + g, rs, :] for g in range(SSD_GROUPS)]
        a = acs[rs]
        xw = (xs * w_exp[rs]).astype(BF16)
        chunk_decay = chunk_decays[8 * c:8 * c + 1]
        a_t = a.T
        dt_t = dt[rs].T
        x_c = xs.astype(BF16)
        state_b = ssd_state.astype(BF16)
        drain(tasks, per_chunk // 2)
        ys, new_state = [], []
        for g in range(SSD_GROUPS):
            b_g = bmat[g]
            c_g = cmat[g]
            cb = _mm_nt(c_g.astype(BF16), b_g)
            for hh in range(SSD_HEADS // SSD_GROUPS):
                h = g * (SSD_HEADS // SSD_GROUPS) + hh
                a_col = a[:, h:h + 1]
                decay = jnp.where(tril, jnp.exp(a_col - a_t[h:h + 1, :]), 0.0)
                m_h = cb * decay * dt_t[h:h + 1, :]
                lhs = jnp.concatenate([m_h, c_g * jnp.exp(a_col)], axis=1).astype(BF16)
                ys.append(lhs)
            ws = slice(g * gw, (g + 1) * gw)
            new_state.append(ssd_state[:, ws] * chunk_decay[:, ws] + _mm_tn(b_g, xw[:, ws]))
        ssd_state = jnp.concatenate(new_state, axis=1)
        pair_out = []
        for p in range(SSD_HEADS // 2):
            ps = slice(p * LANES, (p + 1) * LANES)
            rhs = jnp.concatenate([x_c[:, ps], state_b[:, ps]], axis=0)
            both = _mm(jnp.concatenate([ys[2 * p], ys[2 * p + 1]], axis=0), rhs)
            pair_out.append(jnp.where(low_half, both[:CHUNK], both[CHUNK:]))
        y = jnp.concatenate(pair_out, axis=1)
        u = (y + xs * prm(P_DEXP)) * sz_scr[rs, :]
        normed = []
        for g in range(SSD_GROUPS):
            ug = u[:, g * gw:(g + 1) * gw]
            normed.append(ug * lax.rsqrt(jnp.mean(ug * ug, axis=-1, keepdims=True) + EPS))
        brc_scr[rs, :] = (jnp.concatenate(normed, axis=1) * prm(P_SNORM)).astype(BF16)
        drain(tasks, per_chunk - per_chunk // 2)
    drain(tasks, len(tasks))
    ssd_scr[...] = ssd_state
    conv_scr[:, 0:CONV_PAD, :] = conv_scr[:, ts:ts + CONV_PAD, :]

    u_c = _mm(brc_scr[...], wbr_ref[2])
    merged = merged_scr[...] + (gate_ac_scr[...] * u_c + u_c)
    out_ref[0] = x_ref[0] + _mm(merged.astype(BF16), wo_ref[...])


def _to_ya_lanes(w, axis):
    shape = w.shape
    w = w.reshape(*shape[:axis], ATTN_KV_HEADS, 4, ATTN_HEAD_DIM, *shape[axis + 1:])
    return jnp.swapaxes(w, axis, axis + 1).reshape(shape)


def _const_tables():
    lane = np.arange(LANES)
    gmat = ((lane[:, None] // 32) % 2 == (lane[None, :] // 32) % 2).astype(np.float32) / ATTN_HEAD_DIM
    r = np.arange(2 * CHUNK)
    ltri = ((r[:, None] // CHUNK == r[None, :] // CHUNK) & (r[:, None] >= r[None, :])).astype(np.float32)
    emat = np.zeros((LANES, SSD_D_INNER), np.float32)
    for h in range(SSD_HEADS):
        emat[h, h * SSD_HEAD_DIM:(h + 1) * SSD_HEAD_DIM] = 1.0
    return (jnp.asarray(gmat, BF16), jnp.asarray(ltri, BF16), jnp.asarray(emat, BF16))


def _weight_prep_kernel(wt_ref, o_ref):
    c = IN_PROJ_CUTS

    def put(dst, srcs, width, scale=None):
        pieces = [wt_ref[s:s + width, :] for s in srcs]
        v = pieces[0] if len(pieces) == 1 else jnp.concatenate(pieces, axis=0)
        for k in range(v.shape[0] // LANES):
            t = v[k * LANES:(k + 1) * LANES, :].T
            o_ref[:, dst + k * LANES:dst + (k + 1) * LANES] = (t if scale is None else t * scale).astype(BF16)

    for blk in range(4):
        put(W_Q + blk * LANES, [c[0] + (blk + 4 * (r % 2)) * 64 + 32 * (r // 2) for r in range(4)], 32)
        put(W_GA + blk * LANES, [c[3] + (blk + 4 * g) * 64 for g in range(2)], 64, 0.5)
    put(W_KV, [c[1] + (r % 2) * 64 + 32 * (r // 2) for r in range(4)], 32)
    put(W_KV + LANES, [c[2]], LANES)
    put(W_GQK, [c[4]], 512)
    put(W_GV, [c[6]], 512)
    put(W_GG, [c[7]], 512, 0.5)
    small = jnp.concatenate([wt_ref[c[10]:c[11], :], wt_ref[c[8]:c[9], :],
                             jnp.zeros((LANES - 24, wt_ref.shape[1]), F32)], axis=0)
    o_ref[:, W_SM:W_SM + LANES] = small.T.astype(BF16)
    put(W_XBC, [c[9]], 1024)
    put(W_Z, [c[11]], 512, 0.5)
    put(W_M, [c[12]], 3 * D_MODEL, 0.5)


def _prep_in_proj(w_in):
    depth, d_in, n = w_in.shape
    return pl.pallas_call(
        _weight_prep_kernel,
        grid=(depth, d_in // PREP_ROWS),
        in_specs=[pl.BlockSpec((None, n, PREP_ROWS), lambda l, r: (l, 0, r))],
        out_specs=pl.BlockSpec((None, PREP_ROWS, W_END), lambda l, r: (l, r, 0)),
        out_shape=jax.ShapeDtypeStruct((depth, d_in, W_END), BF16),
        compiler_params=pltpu.CompilerParams(vmem_limit_bytes=VMEM_LIMIT_BYTES),
        name="weight_prep",
    )(jnp.swapaxes(w_in, 1, 2))


def _prep_params(norm_g, w_in, attn_q_norm, attn_k_norm, gla_w_gate_up, gla_b_gate, gla_out_norm,
                 ssd_conv_w, ssd_conv_b, ssd_dt_bias, ssd_A_log, ssd_D, ssd_out_norm, w_branch, w_out):
    depth = w_in.shape[0]
    w_cat = _prep_in_proj(w_in)
    wbr = (0.5 * jnp.concatenate([_to_ya_lanes(w_branch[:, 0:1], 2), w_branch[:, 1:]], axis=1)).astype(BF16)
    wup = jnp.zeros((depth, LANES, gla_w_gate_up.shape[-1]), F32).at[:, 8:8 + GLA_GATE_RANK].set(
        gla_w_gate_up).astype(BF16)
    lane_gain = lambda g: jnp.repeat(g.reshape(depth, 2, 1, 32), 2, axis=2).reshape(depth, LANES)
    pad = lambda v, n: jnp.pad(v.astype(F32), ((0, 0), (0, n - v.shape[-1])))
    prm = jnp.zeros((depth, P_ROWS, D_MODEL), F32)
    prm = prm.at[:, P_NORM[0]].set(norm_g)
    prm = prm.at[:, P_CONVB[0]].set(ssd_conv_b)
    prm = prm.at[:, P_CONVW[0]:P_CONVW[0] + SSD_CONV].set(ssd_conv_w)
    prm = prm.at[:, P_QGAIN[0]].set(jnp.concatenate([
        lane_gain(attn_q_norm), lane_gain(attn_k_norm), pad(ssd_dt_bias, LANES), pad(ssd_A_log, LANES),
        gla_b_gate, jnp.zeros((depth, D_MODEL - 768), F32)], axis=-1))
    prm = prm.at[:, P_GNORM[0]].set(jnp.concatenate([
        jnp.tile(gla_out_norm, (1, GLA_HEADS)), jnp.repeat(ssd_D, SSD_HEAD_DIM, axis=-1)], axis=-1))
    prm = prm.at[:, P_SNORM[0], :SSD_D_INNER].set(ssd_out_norm)
    return w_cat, wbr, w_out.astype(BF16), wup, prm


def _layer_call(x, cos_t, sin_t, sinks, params, consts, layer, ts):
    bsz, seq, d = x.shape
    tok = lambda w: pl.BlockSpec((1, ts, w), lambda b, j: (b, j, 0))

    def layer_spec(arr):
        nd = arr.ndim - 1
        return pl.BlockSpec((None,) + arr.shape[1:], lambda b, j, _nd=nd: (layer,) + (0,) * _nd,
                            pipeline_mode=pl.Buffered(1))

    def const_spec(arr):
        return pl.BlockSpec(arr.shape, lambda b, j, _nd=arr.ndim: (0,) * _nd, pipeline_mode=pl.Buffered(1))

    rope_spec = pl.BlockSpec((1, ts // ROPE_TOKENS, LANES), lambda b, j: (b, j, 0))
    in_specs = [pl.BlockSpec(memory_space=pltpu.SMEM), tok(d), rope_spec, rope_spec]
    in_specs += [layer_spec(a) for a in params] + [const_spec(a) for a in consts]
    conv_slabs = (SSD_D_INNER + 2 * SSD_GROUPS * SSD_D_STATE) // LANES
    scratch = [
        pltpu.VMEM((ts, d), BF16),
        pltpu.VMEM((ATTN_KV_HEADS, 4 * ts, LANES), BF16),
        pltpu.VMEM((ts + ATTN_BLOCK, LANES), BF16),
        pltpu.VMEM((ts + ATTN_BLOCK, LANES), BF16),
        pltpu.VMEM((ts, 512), F32),
        pltpu.VMEM((ts, 512), F32),
        pltpu.VMEM((ts, 512), BF16),
        pltpu.VMEM((ts, d), F32),
        pltpu.VMEM((conv_slabs, ts + CONV_PAD, LANES), F32),
        pltpu.VMEM((conv_slabs, ts, LANES), F32),
        pltpu.VMEM((GLA_HEADS * GLA_DV, GLA_HEADS * GLA_DK), F32),
        pltpu.VMEM((SSD_D_STATE, SSD_D_INNER), F32),
        pltpu.VMEM((ts, 512), F32),
        pltpu.VMEM((ts, 512), BF16),
        pltpu.VMEM((ts, LANES), F32),
        pltpu.VMEM((ts, 512), F32),
        pltpu.VMEM((ts, 512), F32),
        pltpu.VMEM((ts, 512), F32),
        pltpu.VMEM((ts, d), F32),
        pltpu.VMEM((ts, d), F32),
        pltpu.VMEM((ts, 512), BF16),
        pltpu.VMEM((ts, 512), BF16),
        pltpu.VMEM((ts, LANES), F32),
        pltpu.VMEM((ts, LANES), F32),
    ]
    return pl.pallas_call(
        functools.partial(_layer_kernel, ts=ts, layer=layer),
        grid=(bsz, seq // ts),
        in_specs=in_specs,
        out_specs=pl.BlockSpec((1, ts, d), lambda b, j: (b, j, 0)),
        out_shape=jax.ShapeDtypeStruct(x.shape, x.dtype),
        scratch_shapes=scratch,
        compiler_params=pltpu.CompilerParams(
            dimension_semantics=("arbitrary", "arbitrary"),
            vmem_limit_bytes=VMEM_LIMIT_BYTES),
        name="hybrid_layer",
    )(sinks.astype(F32), x, cos_t, sin_t, *params, *consts)


def _rope_tables(positions):
    bsz, seq = positions.shape
    nfreq = ATTN_HEAD_DIM // 2
    per_row = LANES // nfreq
    inv_freq = ROPE_THETA ** (-jnp.arange(0, ATTN_HEAD_DIM, 2, dtype=F32) / ATTN_HEAD_DIM)
    freq = jnp.tile(inv_freq, per_row).reshape(1, LANES)
    pos = jnp.repeat(positions.reshape(bsz, seq // per_row, per_row), nfreq, axis=2)
    spec = pl.BlockSpec((1, seq // per_row, LANES), lambda b: (b, 0, 0))
    return pl.pallas_call(
        _rope_table_kernel,
        grid=(bsz,),
        in_specs=[spec, pl.BlockSpec((1, LANES), lambda b: (0, 0))],
        out_specs=[spec, spec],
        out_shape=[jax.ShapeDtypeStruct((bsz, seq // per_row, LANES), F32)] * 2,
        name="rope_tables",
    )(pos, freq)


def kernel(x, positions, norm_g, w_in, attn_q_norm, attn_k_norm, attn_sinks, gla_w_gate_up, gla_b_gate,
           gla_out_norm, ssd_conv_w, ssd_conv_b, ssd_dt_bias, ssd_A_log, ssd_D, ssd_out_norm,
           w_branch, w_out):
    seq = x.shape[1]
    ts = min(SEQ_BLOCK, seq)
    assert seq % ts == 0 and ts % (2 * CHUNK) == 0 and x.shape[2] == D_MODEL
    cos_t, sin_t = _rope_tables(positions)
    consts = _const_tables()
    params = _prep_params(norm_g, w_in, attn_q_norm, attn_k_norm, gla_w_gate_up, gla_b_gate, gla_out_norm,
                          ssd_conv_w, ssd_conv_b, ssd_dt_bias, ssd_A_log, ssd_D, ssd_out_norm,
                          w_branch, w_out)
    for layer in range(w_in.shape[0]):
        x = _layer_call(x, cos_t, sin_t, attn_sinks, params, consts, layer, ts)
    return x
```

```python
import functools

import numpy as np
import jax
import jax.numpy as jnp
from jax import lax
from jax.experimental import pallas as pl
from jax.experimental.pallas import tpu as pltpu

F32 = jnp.float32
BF16 = jnp.bfloat16

D_MODEL = 1024
EPS = 1e-6
ROPE_THETA = 10000.0

ATTN_HEADS = 8
ATTN_KV_HEADS = 2
ATTN_HEAD_DIM = 64
ATTN_BLOCK = 128
GLA_HEADS = 4
GLA_DK = 64
GLA_DV = 128
GLA_GATE_RANK = 16
GLA_GATE_NORMALIZER = 16.0
SSD_HEADS = 8
SSD_HEAD_DIM = 64
SSD_GROUPS = 2
SSD_D_STATE = 128
SSD_D_INNER = 512
SSD_CONV = 4

IN_PROJ_SIZES = (512, 128, 128, 512, 256, 256, 512, 512, 16, 1024, 8, 512, 3072)
IN_PROJ_CUTS = tuple(int(v) for v in np.cumsum((0,) + IN_PROJ_SIZES))

LANES = 128
CHUNK = 128
CONV_PAD = 8
SEQ_BLOCK = 512
PREP_ROWS = 256
VMEM_LIMIT_BYTES = 56 * 1024 * 1024
MASK_VALUE = -1e30
MXU_TILE = 256
SOFTMAX_ROWS = 64
ROPE_TOKENS = 4
CONV_STRIDE = 4
CONV_GROUP = 8 * CONV_STRIDE

W_Q, W_KV, W_GA, W_GQK, W_GV, W_GG, W_SM, W_XBC, W_Z, W_M, W_END = (
    0, 512, 768, 1280, 1792, 2304, 2816, 2944, 3968, 4480, 7552)
P_NORM, P_CONVB, P_CONVW = (0, 0, 1024), (1, 0, 1024), (2, 0, 1024)
P_QGAIN, P_KGAIN, P_DTB, P_ALOG, P_BGATE = (6, 0, 128), (6, 128, 128), (6, 256, 128), (6, 384, 128), (6, 512, 256)
P_GNORM, P_DEXP = (7, 0, 512), (7, 512, 512)
P_SNORM = (8, 0, 512)
P_ROWS = 16


def _mm(a, b):
    return jnp.dot(a, b, preferred_element_type=F32)


def _mm_nt(a, b):
    return lax.dot_general(a, b, (((1,), (1,)), ((), ())), preferred_element_type=F32)


def _mm_tn(a, b):
    return lax.dot_general(a, b, (((0,), (0,)), ((), ())), preferred_element_type=F32)


def _silu_of_double(vh):
    return vh * jnp.tanh(vh) + vh


def _softplus(v):
    return jnp.maximum(v, 0.0) + jnp.log(1.0 + jnp.exp(-jnp.abs(v)))


def _split3(v):
    hi = v.astype(BF16)
    r1 = v - hi.astype(F32)
    mid = r1.astype(BF16)
    lo = (r1 - mid.astype(F32)).astype(BF16)
    return hi, mid, lo


def _rope_table_kernel(pos_ref, freq_ref, cos_ref, sin_ref):
    ang = pos_ref[0].astype(F32) * freq_ref[...]
    cos_ref[0] = jnp.cos(ang)
    sin_ref[0] = jnp.sin(ang)


def _layer_kernel(
        sinks_ref, x_ref, cos_ref, sin_ref, w_ref, wbr_ref, wo_ref, wup_ref, prm_ref,
        gmat_ref, ltri_ref, emat_ref,
        out_ref,
        hb_scr, q_scr, k_scr, v_scr, bra_scr, brb_scr, brc_scr, merged_scr, conv_scr, xbc_scr,
        gla_scr, ssd_scr, gqk_scr, gv_scr, psm_scr, sga_scr, sgg_scr, sz_scr, gate_ac_scr, gate_b_scr,
        ya_scr, yb_scr, cos_scr, sin_scr,
        *, ts, layer):
    nblk = ts // ATTN_BLOCK
    nchunk = ts // CHUNK
    j = pl.program_id(1)

    @pl.when(j == 0)
    def _reset_carries():
        k_scr[0:ATTN_BLOCK, :] = jnp.zeros((ATTN_BLOCK, LANES), BF16)
        v_scr[0:ATTN_BLOCK, :] = jnp.zeros((ATTN_BLOCK, LANES), BF16)
        conv_scr[:, 0:CONV_PAD, :] = jnp.zeros((conv_scr.shape[0], CONV_PAD, LANES), F32)
        gla_scr[...] = jnp.zeros(gla_scr.shape, F32)
        ssd_scr[...] = jnp.zeros(ssd_scr.shape, F32)

    def prm(p):
        return prm_ref[p[0]:p[0] + 1, p[1]:p[1] + p[2]]

    def proj_tile(c0, dst_ref, d0, fn=None, width=MXU_TILE):
        def run():
            t = _mm(hb_scr[...], w_ref[:, c0:c0 + width])
            dst_ref[:, d0:d0 + width] = (t if fn is None else fn(t)).astype(dst_ref.dtype)
        return run

    def conv_in_tile(c0):
        def run():
            t = _mm(hb_scr[...], w_ref[:, W_XBC + c0:W_XBC + c0 + MXU_TILE])
            for i in range(MXU_TILE // LANES):
                conv_scr[c0 // LANES + i, CONV_PAD:CONV_PAD + ts, :] = t[:, i * LANES:(i + 1) * LANES]
        return run

    def merge_tile(y_ref, branch, gate_ref, c0, first):
        def run():
            u = _mm(y_ref[...], wbr_ref[branch, :, c0:c0 + MXU_TILE])
            g = gate_ref[:, c0:c0 + MXU_TILE] * u + u
            if first:
                merged_scr[:, c0:c0 + MXU_TILE] = g
            else:
                merged_scr[:, c0:c0 + MXU_TILE] += g
        return run

    def drain(tasks, count):
        for _ in range(min(count, len(tasks))):
            tasks.pop(0)()

    half = ts // 4
    for r in range(4):
        x = x_ref[0, r * half:(r + 1) * half, :]
        ms = jnp.mean(x * x, axis=-1, keepdims=True)
        hb_scr[r * half:(r + 1) * half, :] = (x * lax.rsqrt(ms + EPS) * prm(P_NORM)).astype(BF16)

    lane = lax.broadcasted_iota(jnp.int32, (ATTN_BLOCK, LANES), 1)
    low_half = lane < LANES // 2

    lane_q = lax.broadcasted_iota(jnp.int32, (ts // ROPE_TOKENS, LANES), 1) // (LANES // ROPE_TOKENS)
    for src_ref, dst_scr in ((cos_ref, cos_scr), (sin_ref, sin_scr)):
        compact = src_ref[0]
        for p in range(ROPE_TOKENS):
            one = jnp.where(lane_q == p, compact, 0.0)
            two = one + pltpu.roll(one, LANES // 4, 1)
            dst_scr[pl.ds(p, ts // ROPE_TOKENS, stride=ROPE_TOKENS), :] = two + pltpu.roll(two, LANES // 2, 1)
    lane_ts = lax.broadcasted_iota(jnp.int32, (ts, LANES), 1)
    cos_t = cos_scr[...]
    sin_t = jnp.where(lane_ts < LANES // 2, -sin_scr[...], sin_scr[...])
    gmat = gmat_ref[...]
    akv = jnp.concatenate([_mm(hb_scr[r * half:(r + 1) * half, :], w_ref[:, W_KV:W_GA]) for r in range(4)],
                          axis=0)
    ak = akv[:, :LANES]
    kn = ak * lax.rsqrt(_mm((ak * ak).astype(BF16), gmat) + EPS) * prm(P_KGAIN)
    kr = kn * cos_t + pltpu.roll(kn, LANES // 2, 1) * sin_t
    k_scr[ATTN_BLOCK:ATTN_BLOCK + ts, :] = kr.astype(BF16)
    v_scr[ATTN_BLOCK:ATTN_BLOCK + ts, :] = akv[:, LANES:].astype(BF16)

    aq = jnp.concatenate([_mm(hb_scr[r * half:(r + 1) * half, :], w_ref[:, W_Q:W_KV]) for r in range(4)],
                         axis=0)
    group0 = (lane_ts // 32) % 2 == 0
    qgain = prm(P_QGAIN) * (ATTN_HEAD_DIM ** -0.5)
    for c in range(4):
        qc = aq[:, c * LANES:(c + 1) * LANES]
        qn = qc * lax.rsqrt(_mm((qc * qc).astype(BF16), gmat) + EPS) * qgain
        qr = qn * cos_t + pltpu.roll(qn, LANES // 2, 1) * sin_t
        q_scr[0, c * ts:(c + 1) * ts, :] = jnp.where(group0, qr, 0.0).astype(BF16)
        q_scr[1, c * ts:(c + 1) * ts, :] = jnp.where(group0, 0.0, qr).astype(BF16)

    tasks = [proj_tile(W_SM, psm_scr, 0, width=LANES)]
    tasks += [proj_tile(W_GQK + t * MXU_TILE, gqk_scr, t * MXU_TILE) for t in range(2)]
    tasks += [proj_tile(W_GV + t * MXU_TILE, gv_scr, t * MXU_TILE) for t in range(2)]
    tasks += [conv_in_tile(t * MXU_TILE) for t in range(4)]
    drain(tasks, len(tasks))
    tasks = [proj_tile(W_GA + t * MXU_TILE, sga_scr, t * MXU_TILE, _silu_of_double) for t in range(2)]
    tasks += [proj_tile(W_GG + t * MXU_TILE, sgg_scr, t * MXU_TILE, _silu_of_double) for t in range(2)]
    tasks += [proj_tile(W_Z + t * MXU_TILE, sz_scr, t * MXU_TILE, _silu_of_double) for t in range(2)]
    tasks += [proj_tile(W_M + t * MXU_TILE, gate_ac_scr, t * MXU_TILE, jnp.tanh) for t in range(4)]
    per_blk = -(-len(tasks) // nblk)

    qi = lax.broadcasted_iota(jnp.int32, (ATTN_BLOCK, ATTN_BLOCK), 0)
    ki = lax.broadcasted_iota(jnp.int32, (ATTN_BLOCK, ATTN_BLOCK), 1)
    from_prev = ki > qi
    first_block_bias = jnp.where(j > 0, 0.0, MASK_VALUE)
    for n in range(nblk):
        kb = k_scr[n * ATTN_BLOCK:(n + 2) * ATTN_BLOCK, :]
        vb = v_scr[n * ATTN_BLOCK:(n + 2) * ATTN_BLOCK, :]
        outs = []
        for g in range(ATTN_KV_HEADS):
            lhs = jnp.concatenate(
                [q_scr[g, c * ts + n * ATTN_BLOCK:c * ts + (n + 1) * ATTN_BLOCK, :] for c in range(4)],
                axis=0)
            s_all = _mm_nt(lhs, kb)
            probs, scales = [], []
            for r in range(4 * ATTN_BLOCK // SOFTMAX_ROWS):
                r0 = r * SOFTMAX_ROWS
                q0 = r0 % ATTN_BLOCK
                sink = sinks_ref[layer, 4 * g + r0 // ATTN_BLOCK]
                s_prev = s_all[r0:r0 + SOFTMAX_ROWS, :ATTN_BLOCK]
                if n == 0:
                    s_prev = s_prev + first_block_bias
                prev_sel = from_prev[q0:q0 + SOFTMAX_ROWS]
                s = jnp.where(prev_sel, s_prev, s_all[r0:r0 + SOFTMAX_ROWS, ATTN_BLOCK:])
                m = jnp.maximum(jnp.max(s, axis=1, keepdims=True), sink)
                p = jnp.exp(s - m)
                denom = jnp.sum(p, axis=1, keepdims=True) + jnp.exp(sink - m)
                pb = p.astype(BF16)
                zero = jnp.zeros_like(pb)
                probs.append(jnp.concatenate(
                    [jnp.where(prev_sel, pb, zero), jnp.where(prev_sel, zero, pb)], axis=1))
                scales.append(1.0 / denom)
            outs.append(_mm(jnp.concatenate(probs, axis=0), vb) * jnp.concatenate(scales, axis=0))
            drain(tasks, (per_blk + 1 - g) // 2)
        for c in range(4):
            bra_scr[n * ATTN_BLOCK:(n + 1) * ATTN_BLOCK, c * LANES:(c + 1) * LANES] = jnp.where(
                low_half,
                outs[0][c * ATTN_BLOCK:(c + 1) * ATTN_BLOCK],
                outs[1][c * ATTN_BLOCK:(c + 1) * ATTN_BLOCK])
    drain(tasks, len(tasks))
    k_scr[0:ATTN_BLOCK, :] = k_scr[ts:ts + ATTN_BLOCK, :]
    v_scr[0:ATTN_BLOCK, :] = v_scr[ts:ts + ATTN_BLOCK, :]
    ya_scr[...] = (bra_scr[...] * sga_scr[...]).astype(BF16)

    psm = psm_scr[...]
    logit = _mm(psm.astype(BF16), wup_ref[...]) + prm(P_BGATE)
    log_alpha = (jnp.minimum(logit, 0.0) - jnp.log(1.0 + jnp.exp(-jnp.abs(logit)))) * (
        1.0 / GLA_GATE_NORMALIZER)
    dt = _softplus(psm + prm(P_DTB))
    adt = dt * (-jnp.exp(prm(P_ALOG)))
    cs = jnp.concatenate([log_alpha, adt], axis=1)
    row_in_chunk = lax.broadcasted_iota(jnp.int32, cs.shape, 0) % CHUNK
    shift = 1
    while shift < CHUNK:
        cs = cs + jnp.where(row_in_chunk >= shift, pltpu.roll(cs, shift, 0), 0.0)
        shift *= 2
    bcum = cs[:, :GLA_HEADS * GLA_DK]
    acs = cs[:, GLA_HEADS * GLA_DK:]

    tasks = [merge_tile(ya_scr, 0, gate_ac_scr, t * MXU_TILE, True) for t in range(4)]
    tasks += [proj_tile(W_M + D_MODEL + t * MXU_TILE, gate_b_scr, t * MXU_TILE, jnp.tanh)
              for t in range(4)]
    per_chunk = -(-len(tasks) // nchunk)

    dkw = GLA_HEADS * GLA_DK
    lane_k = lax.broadcasted_iota(jnp.int32, (CHUNK, dkw), 1) // GLA_DK
    ci = lax.broadcasted_iota(jnp.int32, (GLA_HEADS * CHUNK, CHUNK), 0) % CHUNK
    cj = lax.broadcasted_iota(jnp.int32, (GLA_HEADS * CHUNK, CHUNK), 1)
    causal4 = ci >= cj
    st_blockdiag = (lax.broadcasted_iota(jnp.int32, gla_scr.shape, 0) // GLA_DV
                    == lax.broadcasted_iota(jnp.int32, gla_scr.shape, 1) // GLA_DK)
    gla_state = gla_scr[...]
    for c in range(nchunk):
        rs = slice(c * CHUNK, (c + 1) * CHUNK)
        bc = bcum[rs]
        bmid = bc[CHUNK // 2 - 1:CHUNK // 2]
        blast = bc[CHUNK - 1:CHUNK]
        q = gqk_scr[rs, :dkw] * (GLA_DK ** -0.5)
        k = gqk_scr[rs, dkw:]
        v = gv_scr[rs, :]
        q_dec = (q * jnp.exp(bc)).astype(BF16)
        q_mid = q * jnp.exp(bc - bmid)
        k_mid = (k * jnp.exp(bmid - bc)).astype(BF16)
        k_end = (k * jnp.exp(blast - bc)).astype(BF16)
        attn = jnp.concatenate(
            [jnp.where(causal4[:CHUNK], _mm_nt(jnp.where(lane_k == h, q_mid, 0.0).astype(BF16), k_mid), 0.0)
             for h in range(GLA_HEADS)], axis=0).astype(BF16)
        o_inter = _mm_nt(q_dec, gla_state.astype(BF16))
        drain(tasks, per_chunk // 2)
        for h in range(GLA_HEADS):
            hs = slice(h * GLA_DV, (h + 1) * GLA_DV)
            o = _mm(attn[h * CHUNK:(h + 1) * CHUNK], v[:, hs]) + o_inter[:, hs]
            brb_scr[rs, hs] = o * lax.rsqrt(jnp.mean(o * o, axis=-1, keepdims=True) + EPS)
        gla_state = jnp.where(st_blockdiag, gla_state * jnp.exp(blast) + _mm_tn(v, k_end), 0.0)
        drain(tasks, per_chunk - per_chunk // 2)
    drain(tasks, len(tasks))
    gla_scr[...] = gla_state
    yb_scr[...] = (brb_scr[...] * prm(P_GNORM) * sgg_scr[...]).astype(BF16)

    tasks = [merge_tile(yb_scr, 1, gate_b_scr, t * MXU_TILE, False) for t in range(4)]
    tasks += [proj_tile(W_M + 2 * D_MODEL + t * MXU_TILE, gate_ac_scr, t * MXU_TILE, jnp.tanh)
              for t in range(4)]
    per_chunk = -(-len(tasks) // nchunk)

    emat = emat_ref[...]
    tril = (lax.broadcasted_iota(jnp.int32, (CHUNK, CHUNK), 0)
            >= lax.broadcasted_iota(jnp.int32, (CHUNK, CHUNK), 1))
    gw = SSD_D_INNER // SSD_GROUPS
    ssd_state = ssd_scr[...]
    a_last_rows = jnp.concatenate(
        [jnp.broadcast_to(acs[(c + 1) * CHUNK - 1:(c + 1) * CHUNK], (CHUNK, LANES)) for c in range(nchunk)], axis=0)
    w_all = dt * jnp.exp(a_last_rows - acs)
    w_hi = w_all.astype(BF16)
    w_lo = (w_all - w_hi.astype(F32)).astype(BF16)
    w_exp = _mm(w_hi, emat) + _mm(w_lo, emat)
    al3 = _split3(jnp.concatenate(
        [jnp.broadcast_to(acs[(c + 1) * CHUNK - 1:(c + 1) * CHUNK], (8, LANES)) for c in range(nchunk)], axis=0))
    chunk_decays = jnp.exp(_mm(al3[0], emat) + _mm(al3[1], emat) + _mm(al3[2], emat))
    nslab = conv_scr.shape[0]
    row = lambda r, sl: jnp.broadcast_to(prm_ref[r:r + 1, sl * LANES:(sl + 1) * LANES] * 0.5, (8, LANES))
    half_convb = [row(P_CONVB[0], sl) for sl in range(nslab)]
    half_convw = [[row(P_CONVW[0] + i, sl) for i in range(SSD_CONV)] for sl in range(nslab)]
    for c in range(nchunk):
        rs = slice(c * CHUNK, (c + 1) * CHUNK)
        for sl in range(nslab):
            for grp in range(CHUNK // CONV_GROUP):
                base = CONV_PAD + c * CHUNK + grp * CONV_GROUP
                taps = [conv_scr[sl, pl.ds(base + m, 8, stride=CONV_STRIDE), :]
                        for m in range(1 - SSD_CONV, CONV_STRIDE)]
                for m in range(CONV_STRIDE):
                    acc = half_convb[sl]
                    for i in range(SSD_CONV):
                        acc = acc + half_convw[sl][i] * taps[m + i]
                    xbc_scr[sl, pl.ds(c * CHUNK + grp * CONV_GROUP + m, 8, stride=CONV_STRIDE), :] = (
                        _silu_of_double(acc))
        nx = SSD_D_INNER // LANES
        xs = jnp.concatenate([xbc_scr[sl, rs, :] for sl in range(nx)], axis=1)
        bmat = [xbc_scr[nx + g, rs, :].astype(BF16) for g in range(SSD_GROUPS)]
        cmat = [xbc_scr[nx + SSD_GROUPS + g, rs, :] for g in range(SSD_GROUPS)]
        a = acs[rs]
        xw = (xs * w_exp[rs]).astype(BF16)
        chunk_decay = chunk_decays[8 * c:8 * c + 1]
        a_t = a.T
        dt_t = dt[rs].T
        x_c = xs.astype(BF16)
        state_b = ssd_state.astype(BF16)
        drain(tasks, per_chunk // 2)
        ys, new_state = [], []
        for g in range(SSD_GROUPS):
            b_g = bmat[g]
            c_g = cmat[g]
            cb = _mm_nt(c_g.astype(BF16), b_g)
            for hh in range(SSD_HEADS // SSD_GROUPS):
                h = g * (SSD_HEADS // SSD_GROUPS) + hh
                a_col = a[:, h:h + 1]
                decay = jnp.where(tril, jnp.exp(a_col - a_t[h:h + 1, :]), 0.0)
                m_h = cb * decay * dt_t[h:h + 1, :]
                lhs = jnp.concatenate([m_h, c_g * jnp.exp(a_col)], axis=1).astype(BF16)
                ys.append(lhs)
            ws = slice(g * gw, (g + 1) * gw)
            new_state.append(ssd_state[:, ws] * chunk_decay[:, ws] + _mm_tn(b_g, xw[:, ws]))
        ssd_state = jnp.concatenate(new_state, axis=1)
        pair_out = []
        for p in range(SSD_HEADS // 2):
            ps = slice(p * LANES, (p + 1) * LANES)
            rhs = jnp.concatenate([x_c[:, ps], state_b[:, ps]], axis=0)
            both = _mm(jnp.concatenate([ys[2 * p], ys[2 * p + 1]], axis=0), rhs)
            pair_out.append(jnp.where(low_half, both[:CHUNK], both[CHUNK:]))
        y = jnp.concatenate(pair_out, axis=1)
        u = (y + xs * prm(P_DEXP)) * sz_scr[rs, :]
        normed = []
        for g in range(SSD_GROUPS):
            ug = u[:, g * gw:(g + 1) * gw]
            normed.append(ug * lax.rsqrt(jnp.mean(ug * ug, axis=-1, keepdims=True) + EPS))
        brc_scr[rs, :] = (jnp.concatenate(normed, axis=1) * prm(P_SNORM)).astype(BF16)
        drain(tasks, per_chunk - per_chunk // 2)
    drain(tasks, len(tasks))
    ssd_scr[...] = ssd_state
    conv_scr[:, 0:CONV_PAD, :] = conv_scr[:, ts:ts + CONV_PAD, :]

    u_c = _mm(brc_scr[...], wbr_ref[2])
    merged = merged_scr[...] + (gate_ac_scr[...] * u_c + u_c)
    out_ref[0] = x_ref[0] + _mm(merged.astype(BF16), wo_ref[...])


def _to_ya_lanes(w, axis):
    shape = w.shape
    w = w.reshape(*shape[:axis], ATTN_KV_HEADS, 4, ATTN_HEAD_DIM, *shape[axis + 1:])
    return jnp.swapaxes(w, axis, axis + 1).reshape(shape)


def _const_tables():
    lane = np.arange(LANES)
    gmat = ((lane[:, None] // 32) % 2 == (lane[None, :] // 32) % 2).astype(np.float32) / ATTN_HEAD_DIM
    r = np.arange(2 * CHUNK)
    ltri = ((r[:, None] // CHUNK == r[None, :] // CHUNK) & (r[:, None] >= r[None, :])).astype(np.float32)
    emat = np.zeros((LANES, SSD_D_INNER), np.float32)
    for h in range(SSD_HEADS):
        emat[h, h * SSD_HEAD_DIM:(h + 1) * SSD_HEAD_DIM] = 1.0
    return (jnp.asarray(gmat, BF16), jnp.asarray(ltri, BF16), jnp.asarray(emat, BF16))


def _weight_prep_kernel(wt_ref, o_ref):
    c = IN_PROJ_CUTS

    def put(dst, srcs, width, scale=None):
        pieces = [wt_ref[s:s + width, :] for s in srcs]
        v = pieces[0] if len(pieces) == 1 else jnp.concatenate(pieces, axis=0)
        for k in range(v.shape[0] // LANES):
            t = v[k * LANES:(k + 1) * LANES, :].T
            o_ref[:, dst + k * LANES:dst + (k + 1) * LANES] = (t if scale is None else t * scale).astype(BF16)

    for blk in range(4):
        put(W_Q + blk * LANES, [c[0] + (blk + 4 * (r % 2)) * 64 + 32 * (r // 2) for r in range(4)], 32)
        put(W_GA + blk * LANES, [c[3] + (blk + 4 * g) * 64 for g in range(2)], 64, 0.5)
    put(W_KV, [c[1] + (r % 2) * 64 + 32 * (r // 2) for r in range(4)], 32)
    put(W_KV + LANES, [c[2]], LANES)
    put(W_GQK, [c[4]], 512)
    put(W_GV, [c[6]], 512)
    put(W_GG, [c[7]], 512, 0.5)
    small = jnp.concatenate([wt_ref[c[10]:c[11], :], wt_ref[c[8]:c[9], :],
                             jnp.zeros((LANES - 24, wt_ref.shape[1]), F32)], axis=0)
    o_ref[:, W_SM:W_SM + LANES] = small.T.astype(BF16)
    put(W_XBC, [c[9]], 1024)
    put(W_Z, [c[11]], 512, 0.5)
    put(W_M, [c[12]], 3 * D_MODEL, 0.5)


def _prep_in_proj(w_in):
    depth, d_in, n = w_in.shape
    return pl.pallas_call(
        _weight_prep_kernel,
        grid=(depth, d_in // PREP_ROWS),
        in_specs=[pl.BlockSpec((None, n, PREP_ROWS), lambda l, r: (l, 0, r))],
        out_specs=pl.BlockSpec((None, PREP_ROWS, W_END), lambda l, r: (l, r, 0)),
        out_shape=jax.ShapeDtypeStruct((depth, d_in, W_END), BF16),
        compiler_params=pltpu.CompilerParams(vmem_limit_bytes=VMEM_LIMIT_BYTES),
        name="weight_prep",
    )(jnp.swapaxes(w_in, 1, 2))


def _prep_params(norm_g, w_in, attn_q_norm, attn_k_norm, gla_w_gate_up, gla_b_gate, gla_out_norm,
                 ssd_conv_w, ssd_conv_b, ssd_dt_bias, ssd_A_log, ssd_D, ssd_out_norm, w_branch, w_out):
    depth = w_in.shape[0]
    w_cat = _prep_in_proj(w_in)
    wbr = (0.5 * jnp.concatenate([_to_ya_lanes(w_branch[:, 0:1], 2), w_branch[:, 1:]], axis=1)).astype(BF16)
    wup = jnp.zeros((depth, LANES, gla_w_gate_up.shape[-1]), F32).at[:, 8:8 + GLA_GATE_RANK].set(
        gla_w_gate_up).astype(BF16)
    lane_gain = lambda g: jnp.repeat(g.reshape(depth, 2, 1, 32), 2, axis=2).reshape(depth, LANES)
    pad = lambda v, n: jnp.pad(v.astype(F32), ((0, 0), (0, n - v.shape[-1])))
    prm = jnp.zeros((depth, P_ROWS, D_MODEL), F32)
    prm = prm.at[:, P_NORM[0]].set(norm_g)
    prm = prm.at[:, P_CONVB[0]].set(ssd_conv_b)
    prm = prm.at[:, P_CONVW[0]:P_CONVW[0] + SSD_CONV].set(ssd_conv_w)
    prm = prm.at[:, P_QGAIN[0]].set(jnp.concatenate([
        lane_gain(attn_q_norm), lane_gain(attn_k_norm), pad(ssd_dt_bias, LANES), pad(ssd_A_log, LANES),
        gla_b_gate, jnp.zeros((depth, D_MODEL - 768), F32)], axis=-1))
    prm = prm.at[:, P_GNORM[0]].set(jnp.concatenate([
        jnp.tile(gla_out_norm, (1, GLA_HEADS)), jnp.repeat(ssd_D, SSD_HEAD_DIM, axis=-1)], axis=-1))
    prm = prm.at[:, P_SNORM[0], :SSD_D_INNER].set(ssd_out_norm)
    return w_cat, wbr, w_out.astype(BF16), wup, prm


def _layer_call(x, cos_t, sin_t, sinks, params, consts, layer, ts):
    bsz, seq, d = x.shape
    tok = lambda w: pl.BlockSpec((1, ts, w), lambda b, j: (b, j, 0))

    def layer_spec(arr):
        nd = arr.ndim - 1
        return pl.BlockSpec((None,) + arr.shape[1:], lambda b, j, _nd=nd: (layer,) + (0,) * _nd,
                            pipeline_mode=pl.Buffered(1))

    def const_spec(arr):
        return pl.BlockSpec(arr.shape, lambda b, j, _nd=arr.ndim: (0,) * _nd, pipeline_mode=pl.Buffered(1))

    rope_spec = pl.BlockSpec((1, ts // ROPE_TOKENS, LANES), lambda b, j: (b, j, 0))
    in_specs = [pl.BlockSpec(memory_space=pltpu.SMEM), tok(d), rope_spec, rope_spec]
    in_specs += [layer_spec(a) for a in params] + [const_spec(a) for a in consts]
    conv_slabs = (SSD_D_INNER + 2 * SSD_GROUPS * SSD_D_STATE) // LANES
    scratch = [
        pltpu.VMEM((ts, d), BF16),
        pltpu.VMEM((ATTN_KV_HEADS, 4 * ts, LANES), BF16),
        pltpu.VMEM((ts + ATTN_BLOCK, LANES), BF16),
        pltpu.VMEM((ts + ATTN_BLOCK, LANES), BF16),
        pltpu.VMEM((ts, 512), F32),
        pltpu.VMEM((ts, 512), F32),
        pltpu.VMEM((ts, 512), BF16),
        pltpu.VMEM((ts, d), F32),
        pltpu.VMEM((conv_slabs, ts + CONV_PAD, LANES), F32),
        pltpu.VMEM((conv_slabs, ts, LANES), F32),
        pltpu.VMEM((GLA_HEADS * GLA_DV, GLA_HEADS * GLA_DK), F32),
        pltpu.VMEM((SSD_D_STATE, SSD_D_INNER), F32),
        pltpu.VMEM((ts, 512), F32),
        pltpu.VMEM((ts, 512), BF16),
        pltpu.VMEM((ts, LANES), F32),
        pltpu.VMEM((ts, 512), F32),
        pltpu.VMEM((ts, 512), F32),
        pltpu.VMEM((ts, 512), F32),
        pltpu.VMEM((ts, d), F32),
        pltpu.VMEM((ts, d), F32),
        pltpu.VMEM((ts, 512), BF16),
        pltpu.VMEM((ts, 512), BF16),
        pltpu.VMEM((ts, LANES), F32),
        pltpu.VMEM((ts, LANES), F32),
    ]
    return pl.pallas_call(
        functools.partial(_layer_kernel, ts=ts, layer=layer),
        grid=(bsz, seq // ts),
        in_specs=in_specs,
        out_specs=pl.BlockSpec((1, ts, d), lambda b, j: (b, j, 0)),
        out_shape=jax.ShapeDtypeStruct(x.shape, x.dtype),
        scratch_shapes=scratch,
        compiler_params=pltpu.CompilerParams(
            dimension_semantics=("arbitrary", "arbitrary"),
            vmem_limit_bytes=VMEM_LIMIT_BYTES),
        name="hybrid_layer",
    )(sinks.astype(F32), x, cos_t, sin_t, *params, *consts)


def _rope_tables(positions):
    bsz, seq = positions.shape
    nfreq = ATTN_HEAD_DIM // 2
    per_row = LANES // nfreq
    inv_freq = ROPE_THETA ** (-jnp.arange(0, ATTN_HEAD_DIM, 2, dtype=F32) / ATTN_HEAD_DIM)
    freq = jnp.tile(inv_freq, per_row).reshape(1, LANES)
    pos = jnp.repeat(positions.reshape(bsz, seq // per_row, per_row), nfreq, axis=2)
    spec = pl.BlockSpec((1, seq // per_row, LANES), lambda b: (b, 0, 0))
    return pl.pallas_call(
        _rope_table_kernel,
        grid=(bsz,),
        in_specs=[spec, pl.BlockSpec((1, LANES), lambda b: (0, 0))],
        out_specs=[spec, spec],
        out_shape=[jax.ShapeDtypeStruct((bsz, seq // per_row, LANES), F32)] * 2,
        name="rope_tables",
    )(pos, freq)


def kernel(x, positions, norm_g, w_in, attn_q_norm, attn_k_norm, attn_sinks, gla_w_gate_up, gla_b_gate,
           gla_out_norm, ssd_conv_w, ssd_conv_b, ssd_dt_bias, ssd_A_log, ssd_D, ssd_out_norm,
           w_branch, w_out):
    seq = x.shape[1]
    ts = min(SEQ_BLOCK, seq)
    assert seq % ts == 0 and ts % (2 * CHUNK) == 0 and x.shape[2] == D_MODEL
    cos_t, sin_t = _rope_tables(positions)
    consts = _const_tables()
    params = _prep_params(norm_g, w_in, attn_q_norm, attn_k_norm, gla_w_gate_up, gla_b_gate, gla_out_norm,
                          ssd_conv_w, ssd_conv_b, ssd_dt_bias, ssd_A_log, ssd_D, ssd_out_norm,
                          w_branch, w_out)
    for layer in range(w_in.shape[0]):
        x = _layer_call(x, cos_t, sin_t, attn_sinks, params, consts, layer, ts)
    return x
```
